```python
import math
import jax, jax.numpy as jnp
from jax import lax
import numpy as np

D_MODEL = 2048
BATCH = 2
SEQ = 4096
DEPTH = 4
DEC_BATCH = 8
DEC_SEQ = 4
PAST_LEN = 16384
PAGE_SIZE = 128

N_META = 16
H_ATT = 8
HD_ATT = 128
D_ATT = H_ATT * HD_ATT
H_SSM = 16
HD_SSM = 64
D_SSM = H_SSM * HD_SSM
G_SSM = 2
R_SSM = H_SSM // G_SSM
N_STATE = 128
CONV_W = 4
C_CONV = D_SSM + 2 * G_SSM * N_STATE
D_MIX = D_ATT + D_SSM
D_IN = 3 * D_ATT + H_ATT + D_SSM + C_CONV + H_SSM
SPLITS = [D_ATT, 2 * D_ATT, 3 * D_ATT, 3 * D_ATT + H_ATT, 3 * D_ATT + H_ATT + D_SSM, 3 * D_ATT + H_ATT + D_SSM + C_CONV]
D_FF = ((8 * D_MODEL + 3 * 256 - 1) // (3 * 256)) * 256
Q_BLOCK = 128
CHUNK = 128
EPS = 1e-6
ATT_SCALE = HD_ATT ** -0.5
FORGET_BIAS = 3.0

kernel_name = 'fox_ssd_hymba_decoder_step'


def rmsnorm(x, g):
    xf = x.astype(jnp.float32)
    y = xf * lax.rsqrt(jnp.mean(xf * xf, axis=-1, keepdims=True) + EPS)
    return (y * g.astype(jnp.float32)).astype(x.dtype)


def project(h, w_in, b_f):
    u = h @ w_in
    q, k, v, f, z, xbc, dt = jnp.split(u, SPLITS, axis=-1)
    lead = h.shape[:-1]
    q = q.reshape(*lead, H_ATT, HD_ATT)
    k = k.reshape(*lead, H_ATT, HD_ATT)
    v = v.reshape(*lead, H_ATT, HD_ATT)
    logf = jax.nn.log_sigmoid(f.astype(jnp.float32) + b_f.astype(jnp.float32))
    return q, k, v, logf, z, xbc, dt


def causal_conv(xp, w, b):
    l = xp.shape[1] - (CONV_W - 1)
    acc = xp[:, 0:l] * w[0]
    for j in range(1, CONV_W):
        acc = acc + xp[:, j:j + l] * w[j]
    return jax.nn.silu(acc + b)


def ssd_inputs(xc, dt_raw, dt_bias, a_log):
    xs, bm, cm = jnp.split(xc, [D_SSM, D_SSM + G_SSM * N_STATE], axis=-1)
    lead = xc.shape[:-1]
    xh = xs.reshape(*lead, H_SSM, HD_SSM).astype(jnp.float32)
    bm = bm.reshape(*lead, G_SSM, N_STATE).astype(jnp.float32)
    cm = cm.reshape(*lead, G_SSM, N_STATE).astype(jnp.float32)
    dt = jax.nn.softplus(dt_raw.astype(jnp.float32) + dt_bias.astype(jnp.float32))
    a = -jnp.exp(a_log.astype(jnp.float32))
    return xh, xh * dt[..., None], dt * a, bm, cm


def ssd_chunked(X, A_dt, Bm, Cm):
    b, l = X.shape[:2]
    c = l // CHUNK
    Xc = X.reshape(b, c, CHUNK, G_SSM, R_SSM, HD_SSM)
    Bc = Bm.reshape(b, c, CHUNK, G_SSM, N_STATE)
    Cc = Cm.reshape(b, c, CHUNK, G_SSM, N_STATE)
    A = A_dt.reshape(b, c, CHUNK, G_SSM, R_SSM).transpose(0, 3, 4, 1, 2)
    A_cs = jnp.cumsum(A, axis=-1)
    tril = jnp.tril(jnp.ones((CHUNK, CHUNK), dtype=bool))
    seg = jnp.exp(jnp.where(tril, A_cs[..., :, None] - A_cs[..., None, :], -jnp.inf))
    CB = jnp.einsum('bclgn,bcsgn->bgcls', Cc, Bc)
    y_diag = jnp.einsum('bgrcls,bcsgrp->bclgrp', CB[:, :, None] * seg, Xc)
    decay_in = jnp.exp(A_cs[..., -1:] - A_cs).transpose(0, 3, 4, 1, 2)
    states = jnp.einsum('bcsgn,bcsgrp->bcgrpn', Bc, Xc * decay_in[..., None])
    chunk_decay = jnp.exp(A_cs[..., -1])

    def step(s, inp):
        st, dec = inp
        return s * dec[..., None, None] + st, s

    init = jnp.zeros((b, G_SSM, R_SSM, HD_SSM, N_STATE), jnp.float32)
    final, prev = lax.scan(step, init, (jnp.moveaxis(states, 1, 0), jnp.moveaxis(chunk_decay, -1, 0)))
    prev = jnp.moveaxis(prev, 0, 1)
    decay_out = jnp.exp(A_cs).transpose(0, 3, 4, 1, 2)
    y_off = jnp.einsum('bclgn,bcgrpn->bclgrp', Cc, prev) * decay_out[..., None]
    y = (y_diag + y_off).reshape(b, l, H_SSM, HD_SSM)
    return y, final.reshape(b, H_SSM, HD_SSM, N_STATE)


def ssd_recurrent(state, X, A_dt, Bm, Cm):
    Bh = jnp.repeat(Bm, R_SSM, axis=2)
    Ch = jnp.repeat(Cm, R_SSM, axis=2)

    def step(s, inp):
        x_t, a_t, b_t, c_t = inp
        s = s * jnp.exp(a_t)[..., None, None] + x_t[..., None] * b_t[:, :, None, :]
        return s, jnp.einsum('bhpn,bhn->bhp', s, c_t)

    xs = (jnp.moveaxis(X, 1, 0), jnp.moveaxis(A_dt, 1, 0), jnp.moveaxis(Bh, 1, 0), jnp.moveaxis(Ch, 1, 0))
    s_fin, ys = lax.scan(step, state.astype(jnp.float32), xs)
    return jnp.moveaxis(ys, 0, 1), s_fin


def fox_prompt(q, k, v, logf):
    b, L = q.shape[:2]
    F = jnp.cumsum(logf, axis=1)
    nb = -(-L // Q_BLOCK)
    Lp = nb * Q_BLOCK
    qb = jnp.pad(q, ((0, 0), (0, Lp - L), (0, 0), (0, 0))).reshape(b, nb, Q_BLOCK, H_ATT, HD_ATT).swapaxes(0, 1)
    Fq = jnp.pad(F, ((0, 0), (0, Lp - L), (0, 0)), mode='edge').reshape(b, nb, Q_BLOCK, H_ATT).swapaxes(0, 1)
    qpos = jnp.arange(Lp).reshape(nb, Q_BLOCK)
    Fk = F.transpose(0, 2, 1)
    kpos = jnp.arange(L)

    def block(args):
        qi, Fi, pi = args
        s = jnp.einsum('bqhd,bkhd->bhqk', qi, k, preferred_element_type=jnp.float32) * ATT_SCALE
        s = s + Fi.transpose(0, 2, 1)[..., None] - Fk[:, :, None, :]
        s = jnp.where(kpos[None, None, None, :] <= pi[None, None, :, None], s, -jnp.inf)
        p = jax.nn.softmax(s, axis=-1)
        return jnp.einsum('bhqk,bkhd->bqhd', p.astype(v.dtype), v)

    o = lax.map(block, (qb, Fq, qpos))
    return o.swapaxes(0, 1).reshape(b, Lp, H_ATT, HD_ATT)[:, :L]


def fox_sample(q, k, v, logf, kp, vp, lfp):
    T = q.shape[1]
    P = kp.shape[1]
    Fp = jnp.cumsum(lfp.astype(jnp.float32), axis=1)
    Fn = Fp[:, -1:] + jnp.cumsum(logf, axis=1)
    FnT = Fn.transpose(0, 2, 1)
    s_p = jnp.einsum('bqhd,bkhd->bhqk', q, kp, preferred_element_type=jnp.float32) * ATT_SCALE
    s_p = s_p + FnT[..., None] - Fp.transpose(0, 2, 1)[:, :, None, :]
    s_n = jnp.einsum('bqhd,bkhd->bhqk', q, k, preferred_element_type=jnp.float32) * ATT_SCALE
    s_n = s_n + FnT[..., None] - FnT[:, :, None, :]
    tril = jnp.tril(jnp.ones((T, T), dtype=bool))
    s_n = jnp.where(tril, s_n, -jnp.inf)
    p = jax.nn.softmax(jnp.concatenate([s_p, s_n], axis=-1), axis=-1).astype(v.dtype)
    return jnp.einsum('bhqk,bkhd->bqhd', p[..., :P], vp) + jnp.einsum('bhqk,bkhd->bqhd', p[..., P:], v)


def merge_and_ffn(x, att, xh, y_ssm, z, lp):
    lead = att.shape[:2]
    dtype = x.dtype
    a = rmsnorm(att.reshape(*lead, D_ATT), lp['g_att_out'])
    y = (y_ssm + lp['d_skip'].astype(jnp.float32)[:, None] * xh).reshape(*lead, D_SSM)
    y = (y * jax.nn.silu(z.astype(jnp.float32))).reshape(*lead, G_SSM, D_SSM // G_SSM)
    y = y * lax.rsqrt(jnp.mean(y * y, axis=-1, keepdims=True) + EPS)
    y = y.reshape(*lead, D_SSM) * lp['g_ssm_out'].astype(jnp.float32)
    mixed = jnp.concatenate([a.astype(dtype), y.astype(dtype)], axis=-1) @ lp['w_out']
    x = x + rmsnorm(mixed, lp['g_post_mix'])
    h = rmsnorm(x, lp['g_pre_ffn'])
    f = (jax.nn.silu(h @ lp['w_gate']) * (h @ lp['w_up'])) @ lp['w_down']
    return x + rmsnorm(f, lp['g_post_ffn'])


def prompt_layer(x, lp):
    L = x.shape[1]
    h = rmsnorm(x, lp['g_pre_mix'])
    q, k, v, logf, z, xbc, dt = project(h, lp['w_in'], lp['b_f'])
    att = fox_prompt(q, k, v, logf)
    conv_tail = xbc[:, L - (CONV_W - 1):]
    xc = causal_conv(jnp.pad(xbc, ((0, 0), (CONV_W - 1, 0), (0, 0))), lp['conv_w'], lp['conv_b'])
    xh, X, A_dt, Bm, Cm = ssd_inputs(xc, dt, lp['dt_bias'], lp['a_log'])
    front = (-N_META) % CHUNK
    back = (-(front + L)) % CHUNK
    padt = lambda a: jnp.pad(a, [(0, 0), (front, back)] + [(0, 0)] * (a.ndim - 2))
    y_ssm, s_fin = ssd_chunked(padt(X), padt(A_dt), padt(Bm), padt(Cm))
    y_ssm = y_ssm[:, front:front + L]
    x = merge_and_ffn(x, att, xh, y_ssm, z, lp)
    return x, (k, v, logf, s_fin, conv_tail)


def sample_layer(x, lp, kp, vp, lfp, s_ssm, s_conv):
    h = rmsnorm(x, lp['g_pre_mix'])
    q, k, v, logf, z, xbc, dt = project(h, lp['w_in'], lp['b_f'])
    att = fox_sample(q, k, v, logf, kp, vp, lfp)
    xp = jnp.concatenate([s_conv.astype(xbc.dtype), xbc], axis=1)
    conv_new = xp[:, xp.shape[1] - (CONV_W - 1):]
    xc = causal_conv(xp, lp['conv_w'], lp['conv_b'])
    xh, X, A_dt, Bm, Cm = ssd_inputs(xc, dt, lp['dt_bias'], lp['a_log'])
    y_ssm, s_new = ssd_recurrent(s_ssm, X, A_dt, Bm, Cm)
    x = merge_and_ffn(x, att, xh, y_ssm, z, lp)
    return x, (k, v, logf, s_new, conv_new)


def setup_inputs(seed: int = 0) -> dict:
    key = jax.random.key(seed)
    ks = jax.random.split(key, 32)
    n_pages = PAST_LEN // PAGE_SIZE
    n_used = DEC_BATCH * n_pages
    n_pool = n_used + max(1, n_used // 4)
    nrm = lambda k, shape, scale: scale * jax.random.normal(k, shape, jnp.float32)
    page_table = jax.random.permutation(ks[0], n_pool)[:n_used].reshape(DEC_BATCH, n_pages).astype(jnp.int32)
    dt0 = jnp.exp(jax.random.uniform(ks[1], (DEPTH, H_SSM), jnp.float32, minval=math.log(1e-3), maxval=math.log(1e-1)))
    dt_bias = dt0 + jnp.log(-jnp.expm1(-dt0))
    a_log = jnp.log(jax.random.uniform(ks[2], (DEPTH, H_SSM), jnp.float32, minval=1.0, maxval=16.0))
    gain = lambda k, n: 1.0 + nrm(k, (DEPTH, n), 0.05)
    return {
        'x_prompt': nrm(ks[3], (BATCH, SEQ, D_MODEL), 1.0),
        'x_sample': nrm(ks[4], (DEC_BATCH, DEC_SEQ, D_MODEL), 1.0),
        'cache_k': nrm(ks[5], (DEPTH, n_pool, PAGE_SIZE, H_ATT, HD_ATT), 1.0),
        'cache_v': nrm(ks[6], (DEPTH, n_pool, PAGE_SIZE, H_ATT, HD_ATT), 1.0),
        'cache_logf': jax.nn.log_sigmoid(FORGET_BIAS + nrm(ks[7], (DEPTH, n_pool, PAGE_SIZE, H_ATT), 1.0)),
        'state_ssm': nrm(ks[8], (DEPTH, DEC_BATCH, H_SSM, HD_SSM, N_STATE), 0.5),
        'state_conv': nrm(ks[9], (DEPTH, DEC_BATCH, CONV_W - 1, C_CONV), 1.0),
        'page_table': page_table,
        'meta': nrm(ks[10], (N_META, D_MODEL), 1.0),
        'w_in': nrm(ks[11], (DEPTH, D_MODEL, D_IN), D_MODEL ** -0.5),
        'b_f': FORGET_BIAS + nrm(ks[12], (DEPTH, H_ATT), 0.5),
        'dt_bias': dt_bias,
        'a_log': a_log,
        'd_skip': 1.0 + nrm(ks[13], (DEPTH, H_SSM), 0.1),
        'conv_w': nrm(ks[14], (DEPTH, CONV_W, C_CONV), CONV_W ** -0.5),
        'conv_b': nrm(ks[15], (DEPTH, C_CONV), 0.01),
        'g_pre_mix': gain(ks[16], D_MODEL),
        'g_post_mix': gain(ks[17], D_MODEL),
        'g_att_out': gain(ks[18], D_ATT),
        'g_ssm_out': gain(ks[19], D_SSM),
        'w_out': nrm(ks[20], (DEPTH, D_MIX, D_MODEL), D_MIX ** -0.5),
        'g_pre_ffn': gain(ks[21], D_MODEL),
        'g_post_ffn': gain(ks[22], D_MODEL),
        'w_gate': nrm(ks[23], (DEPTH, D_MODEL, D_FF), D_MODEL ** -0.5),
        'w_up': nrm(ks[24], (DEPTH, D_MODEL, D_FF), D_MODEL ** -0.5),
        'w_down': nrm(ks[25], (DEPTH, D_FF, D_MODEL), D_FF ** -0.5),
    }


def reference(x_prompt, x_sample, cache_k, cache_v, cache_logf, state_ssm, state_conv, page_table, meta,
              w_in, b_f, dt_bias, a_log, d_skip, conv_w, conv_b, g_pre_mix, g_post_mix, g_att_out, g_ssm_out,
              w_out, g_pre_ffn, g_post_ffn, w_gate, w_up, w_down):
    n_seq, n_pages = page_table.shape
    page = cache_k.shape[2]
    b = x_prompt.shape[0]
    xp = jnp.concatenate([jnp.broadcast_to(meta.astype(x_prompt.dtype)[None], (b, N_META, D_MODEL)), x_prompt], axis=1)
    xs = x_sample
    st_p = []
    st_s = []
    for i in range(DEPTH):
        lp = {'w_in': w_in[i], 'b_f': b_f[i], 'dt_bias': dt_bias[i], 'a_log': a_log[i], 'd_skip': d_skip[i],
              'conv_w': conv_w[i], 'conv_b': conv_b[i], 'g_pre_mix': g_pre_mix[i], 'g_post_mix': g_post_mix[i],
              'g_att_out': g_att_out[i], 'g_ssm_out': g_ssm_out[i], 'w_out': w_out[i], 'g_pre_ffn': g_pre_ffn[i],
              'g_post_ffn': g_post_ffn[i], 'w_gate': w_gate[i], 'w_up': w_up[i], 'w_down': w_down[i]}
        xp, sp = prompt_layer(xp, lp)
        kp = cache_k[i][page_table].reshape(n_seq, n_pages * page, H_ATT, HD_ATT)
        vp = cache_v[i][page_table].reshape(n_seq, n_pages * page, H_ATT, HD_ATT)
        lfp = cache_logf[i][page_table].reshape(n_seq, n_pages * page, H_ATT)
        xs, ss = sample_layer(xs, lp, kp, vp, lfp, state_ssm[i], state_conv[i])
        st_p.append(sp)
        st_s.append(ss)
    k_prompt = jnp.stack([t[0] for t in st_p])
    v_prompt = jnp.stack([t[1] for t in st_p])
    logf_prompt = jnp.stack([t[2] for t in st_p])
    ssm_prompt = jnp.stack([t[3] for t in st_p])
    conv_prompt = jnp.stack([t[4] for t in st_p])
    k_sample = jnp.stack([t[0] for t in st_s])
    v_sample = jnp.stack([t[1] for t in st_s])
    logf_sample = jnp.stack([t[2] for t in st_s])
    ssm_sample = jnp.stack([t[3] for t in st_s])
    conv_sample = jnp.stack([t[4] for t in st_s])
    y_prompt = xp[:, N_META:]
    y_sample = xs
    return (y_prompt, y_sample, k_prompt, v_prompt, logf_prompt, k_sample, v_sample, logf_sample, ssm_prompt, ssm_sample, conv_prompt, conv_sample)
```

```python
import functools
import math

import jax
import jax.numpy as jnp
from jax import lax
from jax.experimental import pallas as pl
from jax.experimental.pallas import tpu as pltpu

D_MODEL = 2048
BATCH = 2
SEQ = 4096
DEPTH = 4
DEC_BATCH = 8
DEC_SEQ = 4
PAST_LEN = 16384
PAGE_SIZE = 128
N_META = 16
H_ATT = 8
HD_ATT = 128
D_ATT = H_ATT * HD_ATT
H_SSM = 16
HD_SSM = 64
D_SSM = H_SSM * HD_SSM
G_SSM = 2
R_SSM = H_SSM // G_SSM
N_STATE = 128
CONV_W = 4
C_CONV = D_SSM + 2 * G_SSM * N_STATE
D_MIX = D_ATT + D_SSM
D_FF = ((8 * D_MODEL + 3 * 256 - 1) // (3 * 256)) * 256
EPS = 1e-6
ATT_SCALE = HD_ATT ** -0.5

LANES = 128
SUBLANES = 8
L_VALID = N_META + SEQ
LP = 4352
M_ROWS = BATCH * LP
N_SAMPLE = DEC_BATCH * DEC_SEQ
SAMPLE_LOCAL = 4128
SAMPLE_ROW0 = (BATCH - 1) * LP + SAMPLE_LOCAL
N_PAGES = PAST_LEN // PAGE_SIZE
SMALL_W = LANES
D_IN_PACKED = 3 * D_ATT + D_SSM + C_CONV + SMALL_W
COL_Z = 3 * D_ATT
COL_XBC = COL_Z + D_SSM
COL_SMALL = COL_XBC + C_CONV
DT_LANE0 = H_ATT

TM_PROJ = 256
TM_FFN = 512
TF_FFN = 512
TQ = 256
TK = 256
Q_SSD = 128
VMEM_LIMIT = 56 * 1024 * 1024

F32 = jnp.float32
BF16 = jnp.bfloat16
NEG_INF = float("-inf")


def _nt_dot(a, b):
    return lax.dot_general(a, b, (((1,), (1,)), ((), ())), preferred_element_type=F32)


def _silu(x):
    return x * (1.0 / (1.0 + jnp.exp(-x)))


def _rms(x, g):
    return x * lax.rsqrt(jnp.mean(x * x, axis=-1, keepdims=True) + EPS) * g


def _cumsum_rows(x):
    n = x.shape[0]
    row = lax.broadcasted_iota(jnp.int32, x.shape, 0)
    s = 1
    while s < n:
        x = x + jnp.where(row >= s, pltpu.roll(x, s, 0), 0.0)
        s *= 2
    return x


def _inproj_kernel(x_ref, g_ref, w_ref, b_ref, q_ref, kf_ref, vf_ref, kb_ref, vb_ref, z_ref, xbc_ref,
                   sm_ref, fcs_ref, carry_sc):
    i = pl.program_id(0)
    h = _rms(x_ref[...], g_ref[...]).astype(BF16)

    def mm(c0, c1):
        return jnp.dot(h, w_ref[:, c0:c1], preferred_element_type=F32)

    q_ref[...] = (mm(0, D_ATT) * ATT_SCALE).astype(BF16)
    k = mm(D_ATT, 2 * D_ATT)
    kf_ref[...] = k
    kb_ref[...] = k.astype(BF16)
    v = mm(2 * D_ATT, 3 * D_ATT)
    vf_ref[...] = v
    vb_ref[...] = v.astype(BF16)
    z_ref[...] = mm(COL_Z, COL_XBC)
    xbc_ref[...] = mm(COL_XBC, COL_SMALL)
    u = mm(COL_SMALL, D_IN_PACKED) + b_ref[...]
    t = jnp.log(1.0 + jnp.exp(-jnp.abs(u)))
    lane = lax.broadcasted_iota(jnp.int32, u.shape, 1)
    sm = jnp.where(lane < H_ATT, jnp.minimum(u, 0.0) - t, jnp.maximum(u, 0.0) + t)
    sm_ref[...] = sm

    @pl.when(i % (LP // TM_PROJ) == 0)
    def _():
        carry_sc[...] = jnp.zeros_like(carry_sc)

    cs = _cumsum_rows(sm) + carry_sc[...]
    fcs_ref[...] = cs
    carry_sc[...] = cs[TM_PROJ - 1:TM_PROJ, :]


def _inproj(x, g, w, bias):
    n = M_ROWS // TM_PROJ
    row = lambda w_: pl.BlockSpec((TM_PROJ, w_), lambda i: (i, 0))
    full = lambda a: pl.BlockSpec(a.shape, lambda i: (0,) * a.ndim)
    out_shape = (
        jax.ShapeDtypeStruct((M_ROWS, D_ATT), BF16),
        jax.ShapeDtypeStruct((M_ROWS, D_ATT), F32),
        jax.ShapeDtypeStruct((M_ROWS, D_ATT), F32),
        jax.ShapeDtypeStruct((M_ROWS, D_ATT), BF16),
        jax.ShapeDtypeStruct((M_ROWS, D_ATT), BF16),
        jax.ShapeDtypeStruct((M_ROWS, D_SSM), F32),
        jax.ShapeDtypeStruct((M_ROWS, C_CONV), F32),
        jax.ShapeDtypeStruct((M_ROWS, SMALL_W), F32),
        jax.ShapeDtypeStruct((M_ROWS, SMALL_W), F32),
    )
    return pl.pallas_call(
        _inproj_kernel,
        grid=(n,),
        in_specs=[row(D_MODEL), full(g),
                  pl.BlockSpec(w.shape, lambda i: (0, 0), pipeline_mode=pl.Buffered(1)),
                  full(bias)],
        out_specs=tuple(row(s.shape[1]) for s in out_shape),
        out_shape=out_shape,
        scratch_shapes=[pltpu.VMEM((1, SMALL_W), F32)],
        compiler_params=pltpu.CompilerParams(dimension_semantics=("arbitrary",),
                                             vmem_limit_bytes=VMEM_LIMIT),
        name="inproj",
    )(x, g, w, bias)


def _attn_kernel(q_ref, k_ref, v_ref, fq_ref, fk_ref, o_ref, m_sc, l_sc, acc_sc):
    h = pl.program_id(1)
    qi = pl.program_id(2)
    q = q_ref[...]
    lane = lax.broadcasted_iota(jnp.int32, (TQ, LANES), 1)
    fq = jnp.sum(jnp.where(lane == h, fq_ref[...], 0.0), axis=1, keepdims=True)
    m_sc[...] = jnp.full_like(m_sc, NEG_INF)
    l_sc[...] = jnp.zeros_like(l_sc)
    acc_sc[...] = jnp.zeros_like(acc_sc)
    q0 = qi * TQ
    rows = q0 + lax.broadcasted_iota(jnp.int32, (TQ, TK), 0)
    cols = lax.broadcasted_iota(jnp.int32, (TQ, TK), 1)

    def step(j, masked):
        ks = pl.multiple_of(j * TK, TK)
        kj = k_ref[pl.ds(ks, TK), :]
        vj = v_ref[pl.ds(ks, TK), :]
        s = _nt_dot(q, kj) + (fq - fk_ref[0, :, pl.ds(ks, TK)])
        if masked:
            s = jnp.where(rows >= ks + cols, s, NEG_INF)
        m_prev = m_sc[...]
        m_new = jnp.maximum(m_prev, jnp.max(s, axis=1, keepdims=True))
        alpha = jnp.exp(m_prev - m_new)
        p = jnp.exp(s - m_new)
        l_sc[...] = alpha * l_sc[...] + jnp.sum(p, axis=1, keepdims=True)
        acc_sc[...] = alpha * acc_sc[...] + jnp.dot(p.astype(BF16), vj, preferred_element_type=F32)
        m_sc[...] = m_new

    n_full = qi * (TQ // TK)
    n_all = (qi + 1) * (TQ // TK)

    def body_full(j, c):
        step(j, False)
        return c

    def body_mask(j, c):
        step(j, True)
        return c

    lax.fori_loop(0, n_full, body_full, 0)
    lax.fori_loop(n_full, n_all, body_mask, 0)
    o_ref[...] = acc_sc[...] * (1.0 / l_sc[...])


def _attn_prompt(q_bf, k_bf, v_bf, fcs, fk_t):
    nq = LP // TQ
    return pl.pallas_call(
        _attn_kernel,
        grid=(BATCH, H_ATT, nq),
        in_specs=[
            pl.BlockSpec((TQ, HD_ATT), lambda b, h, i: (b * nq + i, h)),
            pl.BlockSpec((LP, HD_ATT), lambda b, h, i: (b, h)),
            pl.BlockSpec((LP, HD_ATT), lambda b, h, i: (b, h)),
            pl.BlockSpec((TQ, SMALL_W), lambda b, h, i: (b * nq + i, 0)),
            pl.BlockSpec((1, 1, LP), lambda b, h, i: (b * H_ATT + h, 0, 0)),
        ],
        out_specs=pl.BlockSpec((TQ, HD_ATT), lambda b, h, i: (b * nq + i, h)),
        out_shape=jax.ShapeDtypeStruct((M_ROWS, D_ATT), F32),
        scratch_shapes=[pltpu.VMEM((TQ, 1), F32), pltpu.VMEM((TQ, 1), F32), pltpu.VMEM((TQ, HD_ATT), F32)],
        compiler_params=pltpu.CompilerParams(dimension_semantics=("arbitrary",) * 3,
                                             vmem_limit_bytes=VMEM_LIMIT),
        name="attn_prompt",
    )(q_bf, k_bf, v_bf, fcs, fk_t)


def _ssd_kernel(xbc_ref, sm_ref, z_ref, cw_ref, cb_ref, alog_ref, dsk_ref, g_ref, y_ref, st_ref,
                tail_sc, state_sc, y_sc, xd_sc):
    c = pl.program_id(1)
    nc = pl.num_programs(1)
    Q = Q_SSD

    @pl.when(c == 0)
    def _():
        tail_sc[...] = jnp.zeros_like(tail_sc)
        state_sc[...] = jnp.zeros_like(state_sc)

    x = xbc_ref[...]
    pt = tail_sc[...]
    row8 = lax.broadcasted_iota(jnp.int32, (SUBLANES, C_CONV), 0)
    acc = x * cw_ref[CONV_W - 1:CONV_W, :]
    for k in range(1, CONV_W):
        r = pltpu.roll(x, k, 0)
        first = jnp.where(row8 < k, pltpu.roll(pt, k, 0), r[0:SUBLANES])
        shifted = jnp.concatenate([first, r[SUBLANES:]], axis=0)
        acc = acc + shifted * cw_ref[CONV_W - 1 - k:CONV_W - k, :]
    tail_sc[...] = x[Q - SUBLANES:Q]
    xc = _silu(acc + cb_ref[...])
    xs = xc[:, :D_SSM]

    pos = c * Q + lax.broadcasted_iota(jnp.int32, (Q, LANES), 0)
    lane = lax.broadcasted_iota(jnp.int32, (Q, LANES), 1)
    is_dt = (pos < L_VALID) & (lane >= DT_LANE0) & (lane < DT_LANE0 + H_SSM)
    dt = jnp.where(is_dt, sm_ref[...], 0.0)
    a_dt = dt * (-jnp.exp(alog_ref[...]))
    acs = _cumsum_rows(a_dt)
    acs_t = acs.T
    last = acs[Q - 1:Q, :]
    decay_in = jnp.exp(last - acs)
    decay_out = jnp.exp(acs)
    chunk_decay = jnp.exp(last)
    tril = (lax.broadcasted_iota(jnp.int32, (Q, Q), 0) >= lax.broadcasted_iota(jnp.int32, (Q, Q), 1))

    for g in range(G_SSM):
        bg = xc[:, D_SSM + g * N_STATE:D_SSM + (g + 1) * N_STATE].astype(BF16)
        cg = xc[:, D_SSM + (G_SSM + g) * N_STATE:D_SSM + (G_SSM + g + 1) * N_STATE].astype(BF16)
        cb = _nt_dot(cg, bg)
        for r in range(R_SSM):
            hh = g * R_SSM + r
            col = DT_LANE0 + hh
            sl = slice(hh * HD_SSM, (hh + 1) * HD_SSM)
            xdt = xs[:, sl] * dt[:, col:col + 1]
            seg = jnp.exp(jnp.where(tril, acs[:, col:col + 1] - acs_t[col:col + 1, :], NEG_INF))
            y_diag = jnp.dot((cb * seg).astype(BF16), xdt.astype(BF16), preferred_element_type=F32)
            y_off = _nt_dot(cg, state_sc[hh].astype(BF16)) * decay_out[:, col:col + 1]
            y_sc[:, sl] = y_diag + y_off
            xd_sc[:, sl] = xdt * decay_in[:, col:col + 1]
    xd_t = xd_sc[...].T
    for g in range(G_SSM):
        bg = xc[:, D_SSM + g * N_STATE:D_SSM + (g + 1) * N_STATE].astype(BF16)
        for r in range(R_SSM):
            hh = g * R_SSM + r
            col = DT_LANE0 + hh
            new = jnp.dot(xd_t[hh * HD_SSM:(hh + 1) * HD_SSM, :].astype(BF16), bg,
                          preferred_element_type=F32)
            state_sc[hh] = state_sc[hh] * chunk_decay[:, col:col + 1] + new

    y = (y_sc[...] + dsk_ref[...] * xs) * _silu(z_ref[...])
    half = D_SSM // G_SSM
    parts = []
    for g in range(G_SSM):
        yg = y[:, g * half:(g + 1) * half]
        parts.append(yg * lax.rsqrt(jnp.mean(yg * yg, axis=-1, keepdims=True) + EPS))
    y_ref[...] = jnp.concatenate(parts, axis=1) * g_ref[...]

    @pl.when(c == nc - 1)
    def _():
        st_ref[0] = state_sc[...]


def _ssd_prompt(xbc, small, z, conv_w, conv_b, alog_row, dsk_row, g_row):
    nc = LP // Q_SSD
    row = lambda w_: pl.BlockSpec((Q_SSD, w_), lambda b, c: (b * nc + c, 0))
    full = lambda a: pl.BlockSpec(a.shape, lambda b, c: (0,) * a.ndim)
    return pl.pallas_call(
        _ssd_kernel,
        grid=(BATCH, nc),
        in_specs=[row(C_CONV), row(SMALL_W), row(D_SSM), full(conv_w), full(conv_b), full(alog_row),
                  full(dsk_row), full(g_row)],
        out_specs=(row(D_SSM),
                   pl.BlockSpec((1, H_SSM, HD_SSM, N_STATE), lambda b, c: (b, 0, 0, 0))),
        out_shape=(jax.ShapeDtypeStruct((M_ROWS, D_SSM), F32),
                   jax.ShapeDtypeStruct((BATCH, H_SSM, HD_SSM, N_STATE), F32)),
        scratch_shapes=[pltpu.VMEM((SUBLANES, C_CONV), F32),
                        pltpu.VMEM((H_SSM, HD_SSM, N_STATE), F32),
                        pltpu.VMEM((Q_SSD, D_SSM), F32),
                        pltpu.VMEM((Q_SSD, D_SSM), F32)],
        compiler_params=pltpu.CompilerParams(dimension_semantics=("arbitrary", "arbitrary"),
                                             vmem_limit_bytes=VMEM_LIMIT),
        name="ssd_prompt",
    )(xbc, small, z, conv_w, conv_b, alog_row, dsk_row, g_row)


def _outproj_kernel(att_ref, y_ref, x_ref, w_ref, ga_ref, gm_ref, gf_ref, x1_ref, h2_ref):
    a = _rms(att_ref[...], ga_ref[...]).astype(BF16)
    mixed = jnp.dot(a, w_ref[0:D_ATT, :], preferred_element_type=F32)
    mixed = mixed + jnp.dot(y_ref[...].astype(BF16), w_ref[D_ATT:D_MIX, :], preferred_element_type=F32)
    x1 = x_ref[...] + _rms(mixed, gm_ref[...])
    x1_ref[...] = x1
    h2_ref[...] = _rms(x1, gf_ref[...]).astype(BF16)


def _outproj(att, y, x, w, g_att, g_post_mix, g_pre_ffn):
    n = M_ROWS // TM_PROJ
    row = lambda w_: pl.BlockSpec((TM_PROJ, w_), lambda i: (i, 0))
    full = lambda a: pl.BlockSpec(a.shape, lambda i: (0,) * a.ndim)
    return pl.pallas_call(
        _outproj_kernel,
        grid=(n,),
        in_specs=[row(D_ATT), row(D_SSM), row(D_MODEL),
                  pl.BlockSpec(w.shape, lambda i: (0, 0), pipeline_mode=pl.Buffered(1)),
                  full(g_att), full(g_post_mix), full(g_pre_ffn)],
        out_specs=(row(D_MODEL), row(D_MODEL)),
        out_shape=(jax.ShapeDtypeStruct((M_ROWS, D_MODEL), F32),
                   jax.ShapeDtypeStruct((M_ROWS, D_MODEL), BF16)),
        compiler_params=pltpu.CompilerParams(dimension_semantics=("arbitrary",),
                                             vmem_limit_bytes=VMEM_LIMIT),
        name="outproj",
    )(att, y, x, w, g_att, g_post_mix, g_pre_ffn)


def _ffn_kernel(h_ref, wg_ref, wu_ref, wd_ref, x1_ref, g_ref, o_ref, acc_sc):
    j = pl.program_id(1)

    @pl.when(j == 0)
    def _():
        acc_sc[...] = jnp.zeros_like(acc_sc)

    h = h_ref[...]
    gate = jnp.dot(h, wg_ref[...], preferred_element_type=F32)
    up = jnp.dot(h, wu_ref[...], preferred_element_type=F32)
    act = (_silu(gate) * up).astype(BF16)
    acc_sc[...] += jnp.dot(act, wd_ref[...], preferred_element_type=F32)

    @pl.when(j == pl.num_programs(1) - 1)
    def _():
        o_ref[...] = x1_ref[...] + _rms(acc_sc[...], g_ref[...])


def _ffn(h2, wg, wu, wd, x1, g_post_ffn):
    return pl.pallas_call(
        _ffn_kernel,
        grid=(M_ROWS // TM_FFN, D_FF // TF_FFN),
        in_specs=[pl.BlockSpec((TM_FFN, D_MODEL), lambda i, j: (i, 0)),
                  pl.BlockSpec((D_MODEL, TF_FFN), lambda i, j: (0, j)),
                  pl.BlockSpec((D_MODEL, TF_FFN), lambda i, j: (0, j)),
                  pl.BlockSpec((TF_FFN, D_MODEL), lambda i, j: (j, 0)),
                  pl.BlockSpec((TM_FFN, D_MODEL), lambda i, j: (i, 0)),
                  pl.BlockSpec(g_post_ffn.shape, lambda i, j: (0, 0))],
        out_specs=pl.BlockSpec((TM_FFN, D_MODEL), lambda i, j: (i, 0)),
        out_shape=jax.ShapeDtypeStruct((M_ROWS, D_MODEL), F32),
        scratch_shapes=[pltpu.VMEM((TM_FFN, D_MODEL), F32)],
        compiler_params=pltpu.CompilerParams(dimension_semantics=("arbitrary", "arbitrary"),
                                             vmem_limit_bytes=VMEM_LIMIT),
        name="ffn",
    )(h2, wg, wu, wd, x1, g_post_ffn)


NROW_DEC = DEC_SEQ * H_ATT
PAGE_ROWS = PAGE_SIZE * H_ATT
LF_ROWS = PAGE_ROWS // LANES


def _dec_attn_kernel(pt_ref, q_ref, kn_ref, vn_ref, lfc_ref, lfr_ref, kp_ref, vp_ref, lfp_ref, o_ref,
                     m_sc, l_sc, acc_sc, carry_sc, cn_sc):
    del pt_ref
    j = pl.program_id(1)
    q = q_ref[0]

    def update(s, v_bf):
        m_prev = m_sc[...]
        m_new = jnp.maximum(m_prev, jnp.max(s, axis=1, keepdims=True))
        alpha = jnp.exp(m_prev - m_new)
        p = jnp.exp(s - m_new)
        l_sc[...] = alpha * l_sc[...] + jnp.sum(p, axis=1, keepdims=True)
        acc_sc[...] = alpha * acc_sc[...] + jnp.dot(p.astype(BF16), v_bf, preferred_element_type=F32)
        m_sc[...] = m_new

    @pl.when(j == 0)
    def _():
        r = lax.broadcasted_iota(jnp.int32, (NROW_DEC, NROW_DEC), 0)
        c = lax.broadcasted_iota(jnp.int32, (NROW_DEC, NROW_DEC), 1)
        same = (r & (H_ATT - 1)) == (c & (H_ATT - 1))
        keep = same & ((c >> 3) <= (r >> 3))
        cn_col = jnp.sum(jnp.where(keep, lfr_ref[0], 0.0), axis=1, keepdims=True)
        cn_row = jnp.sum(jnp.where(same & ((r >> 3) <= (c >> 3)), lfc_ref[0], 0.0),
                         axis=0, keepdims=True)
        m_sc[...] = jnp.full_like(m_sc, NEG_INF)
        l_sc[...] = jnp.zeros_like(l_sc)
        acc_sc[...] = jnp.zeros_like(acc_sc)
        carry_sc[...] = jnp.zeros_like(carry_sc)
        cn_sc[...] = cn_col
        s = _nt_dot(q, kn_ref[0].astype(BF16)) + cn_col - cn_row
        update(jnp.where(keep, s, NEG_INF), vn_ref[0].astype(BF16))

    lf = lfp_ref[0]
    lane = lax.broadcasted_iota(jnp.int32, lf.shape, 1)
    rowi = lax.broadcasted_iota(jnp.int32, lf.shape, 0)
    suf = lf
    tot = lf
    s_ = H_ATT
    while s_ < LANES:
        suf = suf + jnp.where(lane < LANES - s_, pltpu.roll(suf, LANES - s_, 1), 0.0)
        tot = tot + pltpu.roll(tot, s_, 1)
        s_ *= 2
    rsuf = tot
    s_ = 1
    while s_ < LF_ROWS:
        rsuf = rsuf + jnp.where(rowi < LF_ROWS - s_, pltpu.roll(rsuf, LF_ROWS - s_, 0), 0.0)
        s_ *= 2
    carry = carry_sc[...]
    decay = carry + (suf - lf) + (rsuf - tot)
    carry_sc[...] = carry + rsuf[0:1, :]
    bias = jnp.concatenate([decay[i:i + 1, :] for i in range(LF_ROWS)], axis=1)

    k = kp_ref[0, 0].reshape(PAGE_ROWS, HD_ATT).astype(BF16)
    v = vp_ref[0, 0].reshape(PAGE_ROWS, HD_ATT).astype(BF16)
    s = _nt_dot(q, k) + cn_sc[...] + bias
    r = lax.broadcasted_iota(jnp.int32, (NROW_DEC, PAGE_ROWS), 0)
    c = lax.broadcasted_iota(jnp.int32, (NROW_DEC, PAGE_ROWS), 1)
    update(jnp.where((r & (H_ATT - 1)) == (c & (H_ATT - 1)), s, NEG_INF), v)

    @pl.when(j == pl.num_programs(1) - 1)
    def _():
        o_ref[0] = acc_sc[...] * (1.0 / l_sc[...])


def _dec_attn(layer, page_flat, q_s, kn, vn, lf_col, lf_row, cache_k, cache_v, lfp):
    def page(b, j, pt):
        return pt[b * N_PAGES + (N_PAGES - 1 - j)]

    per_seq = lambda a: pl.BlockSpec((1,) + a.shape[1:], lambda b, j, pt: (b,) + (0,) * (a.ndim - 1))
    grid_spec = pltpu.PrefetchScalarGridSpec(
        num_scalar_prefetch=1,
        grid=(DEC_BATCH, N_PAGES),
        in_specs=[per_seq(q_s), per_seq(kn), per_seq(vn), per_seq(lf_col), per_seq(lf_row),
                  pl.BlockSpec((1, 1, PAGE_SIZE, H_ATT, HD_ATT),
                               lambda b, j, pt: (layer, page(b, j, pt), 0, 0, 0)),
                  pl.BlockSpec((1, 1, PAGE_SIZE, H_ATT, HD_ATT),
                               lambda b, j, pt: (layer, page(b, j, pt), 0, 0, 0)),
                  pl.BlockSpec((1, LF_ROWS, LANES), lambda b, j, pt: (page(b, j, pt), 0, 0))],
        out_specs=pl.BlockSpec((1, NROW_DEC, HD_ATT), lambda b, j, pt: (b, 0, 0)),
        scratch_shapes=[pltpu.VMEM((NROW_DEC, 1), F32), pltpu.VMEM((NROW_DEC, 1), F32),
                        pltpu.VMEM((NROW_DEC, HD_ATT), F32), pltpu.VMEM((1, LANES), F32),
                        pltpu.VMEM((NROW_DEC, 1), F32)],
    )
    return pl.pallas_call(
        _dec_attn_kernel,
        grid_spec=grid_spec,
        out_shape=jax.ShapeDtypeStruct((DEC_BATCH, NROW_DEC, HD_ATT), F32),
        compiler_params=pltpu.CompilerParams(dimension_semantics=("arbitrary", "arbitrary"),
                                             vmem_limit_bytes=VMEM_LIMIT),
        name="dec_attn",
    )(page_flat, q_s, kn, vn, lf_col, lf_row, cache_k, cache_v, lfp)


def _dec_ssm_kernel(xsh_ref, xsht_ref, cw_ref, cwt_ref, cb_ref, cbt_ref, dt_ref, alog_ref, zt_ref,
                    dskt_ref, gt_ref, st_ref, sto_ref, yt_ref, y_sc):
    xc = cb_ref[...]
    xct = cbt_ref[...]
    for j in range(CONV_W):
        xc = xc + xsh_ref[0, j] * cw_ref[j:j + 1, :]
        xct = xct + xsht_ref[0, j] * cwt_ref[:, j:j + 1]
    xc = _silu(xc)
    xct = _silu(xct)
    dt = dt_ref[0]
    da = jnp.exp(dt * (-jnp.exp(alog_ref[...])))
    for hh in range(H_SSM):
        g = hh // R_SSM
        s = st_ref[0, 0, hh]
        for t in range(DEC_SEQ):
            b = xc[t:t + 1, D_SSM + g * N_STATE:D_SSM + (g + 1) * N_STATE]
            c = xc[t:t + 1, D_SSM + (G_SSM + g) * N_STATE:D_SSM + (G_SSM + g + 1) * N_STATE]
            xcol = xct[hh * HD_SSM:(hh + 1) * HD_SSM, t:t + 1]
            s = s * da[t:t + 1, hh:hh + 1] + (xcol * dt[t:t + 1, hh:hh + 1]) * b
            y_sc[hh * HD_SSM:(hh + 1) * HD_SSM, t:t + 1] = jnp.sum(s * c, axis=1, keepdims=True)
        sto_ref[0, hh] = s
    y = (y_sc[...] + dskt_ref[...] * xct[0:D_SSM, :]) * _silu(zt_ref[0])
    half = D_SSM // G_SSM
    parts = []
    for g in range(G_SSM):
        yg = y[g * half:(g + 1) * half, :]
        parts.append(yg * lax.rsqrt(jnp.mean(yg * yg, axis=0, keepdims=True) + EPS))
    yt_ref[0] = jnp.concatenate(parts, axis=0) * gt_ref[...]


def _dec_ssm(layer, xsh, xsht, cw, cwt, cb, cbt, dt_s, alog, zt, dskt, gt, state_ssm):
    per_seq = lambda a: pl.BlockSpec((1,) + a.shape[1:], lambda b: (b,) + (0,) * (a.ndim - 1))
    full = lambda a: pl.BlockSpec(a.shape, lambda b: (0,) * a.ndim)
    return pl.pallas_call(
        _dec_ssm_kernel,
        grid=(DEC_BATCH,),
        in_specs=[per_seq(xsh), per_seq(xsht), full(cw), full(cwt), full(cb), full(cbt), per_seq(dt_s),
                  full(alog), per_seq(zt), full(dskt), full(gt),
                  pl.BlockSpec((1, 1, H_SSM, HD_SSM, N_STATE), lambda b: (layer, b, 0, 0, 0))],
        out_specs=(pl.BlockSpec((1, H_SSM, HD_SSM, N_STATE), lambda b: (b, 0, 0, 0)),
                   pl.BlockSpec((1, D_SSM, DEC_SEQ), lambda b: (b, 0, 0))),
        out_shape=(jax.ShapeDtypeStruct((DEC_BATCH, H_SSM, HD_SSM, N_STATE), F32),
                   jax.ShapeDtypeStruct((DEC_BATCH, D_SSM, DEC_SEQ), F32)),
        scratch_shapes=[pltpu.VMEM((D_SSM, DEC_SEQ), F32)],
        compiler_params=pltpu.CompilerParams(dimension_semantics=("arbitrary",),
                                             vmem_limit_bytes=VMEM_LIMIT),
        name="dec_ssm",
    )(xsh, xsht, cw, cwt, cb, cbt, dt_s, alog, zt, dskt, gt, state_ssm)


def _small_row(*pieces):
    v = jnp.concatenate([p.astype(F32) for p in pieces])
    return jnp.pad(v, (0, SMALL_W - v.shape[0]))[None, :]


def kernel(x_prompt, x_sample, cache_k, cache_v, cache_logf, state_ssm, state_conv, page_table, meta,
           w_in, b_f, dt_bias, a_log, d_skip, conv_w, conv_b, g_pre_mix, g_post_mix, g_att_out, g_ssm_out,
           w_out, g_pre_ffn, g_post_ffn, w_gate, w_up, w_down):
    assert x_prompt.shape == (BATCH, SEQ, D_MODEL) and x_sample.shape == (DEC_BATCH, DEC_SEQ, D_MODEL)
    assert page_table.shape == (DEC_BATCH, N_PAGES)
    n_pool = cache_k.shape[1]

    xp = jnp.concatenate([jnp.broadcast_to(meta.astype(F32)[None], (BATCH, N_META, D_MODEL)), x_prompt,
                          jnp.zeros((BATCH, LP - L_VALID, D_MODEL), F32)], axis=1)
    x = xp.reshape(M_ROWS, D_MODEL)
    x = lax.dynamic_update_slice(x, x_sample.reshape(N_SAMPLE, D_MODEL), (SAMPLE_ROW0, 0))

    page_flat = page_table.reshape(-1).astype(jnp.int32)
    lfp = cache_logf.reshape(DEPTH, n_pool, LF_ROWS, LANES)

    f0, f1 = 3 * D_ATT, 3 * D_ATT + H_ATT
    z0, z1 = f1, f1 + D_SSM
    c0, c1 = z1, z1 + C_CONV
    srows = slice(SAMPLE_ROW0, SAMPLE_ROW0 + N_SAMPLE)

    outs = {k: [] for k in ("k_p", "v_p", "lf_p", "k_s", "v_s", "lf_s", "ssm_p", "ssm_s", "conv_p", "conv_s")}
    for i in range(DEPTH):
        wi = w_in[i]
        w_packed = jnp.concatenate(
            [wi[:, :f0], wi[:, z0:z1], wi[:, c0:c1], wi[:, f0:f1], wi[:, c1:],
             jnp.zeros((D_MODEL, SMALL_W - H_ATT - H_SSM), F32)], axis=1).astype(BF16)
        bias = _small_row(b_f[i], dt_bias[i])
        q_bf, k_f, v_f, k_bf, v_bf, z, xbc, small, fcs = _inproj(x, g_pre_mix[i][None, :], w_packed, bias)

        fk_t = fcs[:, :H_ATT].reshape(BATCH, LP, H_ATT).transpose(0, 2, 1).reshape(BATCH * H_ATT, 1, LP)
        att = _attn_prompt(q_bf, k_bf, v_bf, fcs, fk_t)
        alog_row = jnp.pad(a_log[i].astype(F32), (DT_LANE0, SMALL_W - DT_LANE0 - H_SSM))[None, :]
        dsk_row = jnp.repeat(d_skip[i].astype(F32), HD_SSM)[None, :]
        g_ssm_row = g_ssm_out[i].astype(F32)[None, :]
        y_ssm, st_p = _ssd_prompt(xbc, small, z, conv_w[i], conv_b[i][None, :], alog_row, dsk_row, g_ssm_row)

        q_s = q_bf[srows].reshape(DEC_BATCH, NROW_DEC, HD_ATT)
        kn = k_f[srows].reshape(DEC_BATCH, NROW_DEC, HD_ATT)
        vn = v_f[srows].reshape(DEC_BATCH, NROW_DEC, HD_ATT)
        lf_s = small[srows, :H_ATT].reshape(DEC_BATCH, DEC_SEQ, H_ATT)
        att_s = _dec_attn(i, page_flat, q_s, kn, vn, lf_s.reshape(DEC_BATCH, NROW_DEC, 1),
                          lf_s.reshape(DEC_BATCH, 1, NROW_DEC), cache_k, cache_v, lfp[i])
        xbc_s = xbc[srows].reshape(DEC_BATCH, DEC_SEQ, C_CONV)
        xpad = jnp.concatenate([state_conv[i].astype(F32), xbc_s], axis=1)
        xsh = jnp.stack([xpad[:, j:j + DEC_SEQ] for j in range(CONV_W)], axis=1)
        xsht = xsh.transpose(0, 1, 3, 2)
        dt_s = small[srows, DT_LANE0:DT_LANE0 + H_SSM].reshape(DEC_BATCH, DEC_SEQ, H_SSM)
        zt = z[srows].reshape(DEC_BATCH, DEC_SEQ, D_SSM).transpose(0, 2, 1)
        st_s, y_s_t = _dec_ssm(i, xsh, xsht, conv_w[i], conv_w[i].T, conv_b[i][None, :], conv_b[i][:, None],
                               dt_s, a_log[i].astype(F32)[None, :], zt, dsk_row.T, g_ssm_row.T, state_ssm)
        att = lax.dynamic_update_slice(att, att_s.reshape(N_SAMPLE, D_ATT), (SAMPLE_ROW0, 0))
        y_ssm = lax.dynamic_update_slice(y_ssm, y_s_t.transpose(0, 2, 1).reshape(N_SAMPLE, D_SSM),
                                         (SAMPLE_ROW0, 0))

        x1, h2 = _outproj(att, y_ssm, x, w_out[i].astype(BF16), g_att_out[i][None, :],
                          g_post_mix[i][None, :], g_pre_ffn[i][None, :])
        x = _ffn(h2, w_gate[i].astype(BF16), w_up[i].astype(BF16), w_down[i].astype(BF16), x1,
                 g_post_ffn[i][None, :])

        outs["k_p"].append(k_f.reshape(BATCH, LP, H_ATT, HD_ATT)[:, :L_VALID])
        outs["v_p"].append(v_f.reshape(BATCH, LP, H_ATT, HD_ATT)[:, :L_VALID])
        outs["lf_p"].append(small[:, :H_ATT].reshape(BATCH, LP, H_ATT)[:, :L_VALID])
        outs["k_s"].append(kn.reshape(DEC_BATCH, DEC_SEQ, H_ATT, HD_ATT))
        outs["v_s"].append(vn.reshape(DEC_BATCH, DEC_SEQ, H_ATT, HD_ATT))
        outs["lf_s"].append(lf_s)
        outs["ssm_p"].append(st_p)
        outs["ssm_s"].append(st_s)
        outs["conv_p"].append(xbc.reshape(BATCH, LP, C_CONV)[:, L_VALID - (CONV_W - 1):L_VALID])
        outs["conv_s"].append(xpad[:, DEC_SEQ:])

    xf = x.reshape(BATCH, LP, D_MODEL)
    y_prompt = xf[:, N_META:L_VALID]
    y_sample = x[srows].reshape(DEC_BATCH, DEC_SEQ, D_MODEL)
    st = lambda k: jnp.stack(outs[k])
    return (y_prompt, y_sample, st("k_p"), st("v_p"), st("lf_p"), st("k_s"), st("v_s"), st("lf_s"),
            st("ssm_p"), st("ssm_s"), st("conv_p"), st("conv_s"))
```

```python
import functools
import math

import jax
import jax.numpy as jnp
from jax import lax
from jax.experimental import pallas as pl
from jax.experimental.pallas import tpu as pltpu

D_MODEL = 2048
BATCH = 2
SEQ = 4096
DEPTH = 4
DEC_BATCH = 8
DEC_SEQ = 4
PAST_LEN = 16384
PAGE_SIZE = 128
N_META = 16
H_ATT = 8
HD_ATT = 128
D_ATT = H_ATT * HD_ATT
H_SSM = 16
HD_SSM = 64
D_SSM = H_SSM * HD_SSM
G_SSM = 2
R_SSM = H_SSM // G_SSM
N_STATE = 128
CONV_W = 4
C_CONV = D_SSM + 2 * G_SSM * N_STATE
D_MIX = D_ATT + D_SSM
D_FF = ((8 * D_MODEL + 3 * 256 - 1) // (3 * 256)) * 256
EPS = 1e-6
ATT_SCALE = HD_ATT ** -0.5

LANES = 128
SUBLANES = 8
L_VALID = N_META + SEQ
LP = 4352
M_ROWS = BATCH * LP
N_SAMPLE = DEC_BATCH * DEC_SEQ
SAMPLE_LOCAL = 4128
SAMPLE_ROW0 = (BATCH - 1) * LP + SAMPLE_LOCAL
N_PAGES = PAST_LEN // PAGE_SIZE
SMALL_W = LANES
D_IN_PACKED = 3 * D_ATT + D_SSM + C_CONV + SMALL_W
COL_Z = 3 * D_ATT
COL_XBC = COL_Z + D_SSM
COL_SMALL = COL_XBC + C_CONV
DT_LANE0 = H_ATT

TM_PROJ = 256
TM_FFN = 512
TF_FFN = 512
TQ = 512
TK = 512
TK_DIAG = 256
HEADS_PER_STEP = 2
NQ_FULL = L_VALID // TQ - 1
TQ_LAST = -(-(L_VALID - NQ_FULL * TQ) // LANES) * LANES
ATT_ROWS = NQ_FULL * TQ + TQ_LAST
LOG2E = math.log2(math.e)
Q_SSD = 128
VMEM_LIMIT = 56 * 1024 * 1024

F32 = jnp.float32
BF16 = jnp.bfloat16
NEG_INF = float("-inf")


def _nt_dot(a, b):
    return lax.dot_general(a, b, (((1,), (1,)), ((), ())), preferred_element_type=F32)


def _silu(x):
    return x * (1.0 / (1.0 + jnp.exp(-x)))


def _rms(x, g):
    return x * lax.rsqrt(jnp.mean(x * x, axis=-1, keepdims=True) + EPS) * g


def _cumsum_rows(x):
    n = x.shape[0]
    row = lax.broadcasted_iota(jnp.int32, x.shape, 0)
    s = 1
    while s < n:
        x = x + jnp.where(row >= s, pltpu.roll(x, s, 0), 0.0)
        s *= 2
    return x


def _inproj_kernel(x_ref, g_ref, w_ref, b_ref, q_ref, kb_ref, vb_ref, kp_ref, vp_ref, ks_ref, vs_ref,
                   z_ref, xbc_ref, sm_ref, fcs_ref, carry_sc):
    i = pl.program_id(0)
    h = _rms(x_ref[...], g_ref[...]).astype(BF16)

    def mm(c0, c1):
        return jnp.dot(h, w_ref[0, :, c0:c1], preferred_element_type=F32)

    q_ref[...] = ((mm(0, D_ATT) * ATT_SCALE) * LOG2E).astype(BF16)
    k = mm(D_ATT, 2 * D_ATT)
    kp_ref[0] = k
    kb_ref[...] = k.astype(BF16)
    v = mm(2 * D_ATT, 3 * D_ATT)
    vp_ref[0] = v
    vb_ref[...] = v.astype(BF16)

    @pl.when(i == SAMPLE_ROW0 // TM_PROJ)
    def _():
        r0 = SAMPLE_ROW0 % TM_PROJ
        ks_ref[...] = k[r0:r0 + N_SAMPLE]
        vs_ref[...] = v[r0:r0 + N_SAMPLE]

    z_ref[...] = mm(COL_Z, COL_XBC)
    xbc_ref[...] = mm(COL_XBC, COL_SMALL)
    u = mm(COL_SMALL, D_IN_PACKED) + b_ref[...]
    t = jnp.log(1.0 + jnp.exp(-jnp.abs(u)))
    lane = lax.broadcasted_iota(jnp.int32, u.shape, 1)
    sm = jnp.where(lane < H_ATT, jnp.minimum(u, 0.0) - t, jnp.maximum(u, 0.0) + t)
    sm_ref[...] = sm

    @pl.when(i % (LP // TM_PROJ) == 0)
    def _():
        carry_sc[...] = jnp.zeros_like(carry_sc)

    cs = _cumsum_rows(sm) + carry_sc[...]
    fcs_ref[...] = cs
    carry_sc[...] = cs[TM_PROJ - 1:TM_PROJ, :]


def _inproj(layer, x, g, w, bias):
    n = M_ROWS // TM_PROJ
    row = lambda w_: pl.BlockSpec((TM_PROJ, w_), lambda i: (i, 0))
    full = lambda a: pl.BlockSpec(a.shape, lambda i: (0,) * a.ndim)
    tiles_per_seq = LP // TM_PROJ
    cache = pl.BlockSpec((1, TM_PROJ, D_ATT), lambda i: (i // tiles_per_seq, i % tiles_per_seq, 0))
    sample = pl.BlockSpec((N_SAMPLE, D_ATT), lambda i: (0, 0))
    outs = (
        (jax.ShapeDtypeStruct((M_ROWS, D_ATT), BF16), row(D_ATT)),
        (jax.ShapeDtypeStruct((M_ROWS, D_ATT), BF16), row(D_ATT)),
        (jax.ShapeDtypeStruct((M_ROWS, D_ATT), BF16), row(D_ATT)),
        (jax.ShapeDtypeStruct((BATCH, L_VALID, D_ATT), F32), cache),
        (jax.ShapeDtypeStruct((BATCH, L_VALID, D_ATT), F32), cache),
        (jax.ShapeDtypeStruct((N_SAMPLE, D_ATT), F32), sample),
        (jax.ShapeDtypeStruct((N_SAMPLE, D_ATT), F32), sample),
        (jax.ShapeDtypeStruct((M_ROWS, D_SSM), F32), row(D_SSM)),
        (jax.ShapeDtypeStruct((M_ROWS, C_CONV), F32), row(C_CONV)),
        (jax.ShapeDtypeStruct((M_ROWS, SMALL_W), F32), row(SMALL_W)),
        (jax.ShapeDtypeStruct((M_ROWS, SMALL_W), F32), row(SMALL_W)),
    )
    return pl.pallas_call(
        _inproj_kernel,
        grid=(n,),
        in_specs=[row(D_MODEL), full(g),
                  pl.BlockSpec((1,) + w.shape[1:], lambda i: (layer, 0, 0), pipeline_mode=pl.Buffered(1)),
                  full(bias)],
        out_specs=tuple(o[1] for o in outs),
        out_shape=tuple(o[0] for o in outs),
        scratch_shapes=[pltpu.VMEM((1, SMALL_W), F32)],
        compiler_params=pltpu.CompilerParams(dimension_semantics=("arbitrary",),
                                             vmem_limit_bytes=VMEM_LIMIT),
        name="inproj",
    )(x, g, w, bias)


def _attn_step(hh, ks, tk, q0, lo, w, masked, q_ref, k_ref, fkb_sc, vt_sc, m_sc, l_sc, acc_sc):
    hs = slice(hh * HD_ATT, (hh + 1) * HD_ATT)
    s = _nt_dot(k_ref[pl.ds(ks, tk), hs], q_ref[pl.ds(q0 + lo, w), hs])
    s = s - jnp.concatenate([fkb_sc[hh, pl.ds(ks, tk), :]] * (w // LANES), axis=1)
    if masked:
        s = jnp.where(lax.broadcasted_iota(jnp.int32, (tk, w), 0) <= lax.broadcasted_iota(jnp.int32, (tk, w), 1),
                      s, NEG_INF)
    m_prev = m_sc[hh, :, lo:lo + w]
    m_new = jnp.maximum(m_prev, jnp.max(s, axis=0, keepdims=True))
    alpha = jnp.exp2(m_prev - m_new)
    p = jnp.exp2(s - m_new)
    l_sc[hh, :, lo:lo + w] = alpha * l_sc[hh, :, lo:lo + w] + jnp.sum(p, axis=0, keepdims=True)
    acc_sc[hh, :, lo:lo + w] = alpha * acc_sc[hh, :, lo:lo + w] + jnp.dot(
        vt_sc[hh, :, pl.ds(ks, tk)], p.astype(BF16), preferred_element_type=F32)
    m_sc[hh, :, lo:lo + w] = m_new


def _attn_kernel(q_ref, k_ref, v_ref, fcs_ref, o_ref, fkb_sc, vt_sc, m_sc, l_sc, acc_sc):
    hp = pl.program_id(1)
    lane = lax.broadcasted_iota(jnp.int32, (TK_DIAG, LANES), 1)
    heads = range(HEADS_PER_STEP)

    def prep(c, carry):
        r0 = pl.multiple_of(c * TK_DIAG, TK_DIAG)
        f = fcs_ref[pl.ds(r0, TK_DIAG), :]
        for hh in heads:
            col = jnp.sum(jnp.where(lane == hp * HEADS_PER_STEP + hh, f, 0.0), axis=1, keepdims=True) * LOG2E
            fkb_sc[hh, pl.ds(r0, TK_DIAG), :] = jnp.broadcast_to(col, (TK_DIAG, LANES))
            vt_sc[hh, :, pl.ds(r0, TK_DIAG)] = v_ref[pl.ds(r0, TK_DIAG),
                                                     hh * HD_ATT:(hh + 1) * HD_ATT].astype(F32).T.astype(BF16)
        return carry

    lax.fori_loop(0, LP // TK_DIAG, prep, 0)
    refs = (q_ref, k_ref, fkb_sc, vt_sc, m_sc, l_sc, acc_sc)

    def q_tile(q0, w, n_unmasked):
        for hh in heads:
            m_sc[hh, :, 0:w] = jnp.full((1, w), NEG_INF, F32)
            l_sc[hh, :, 0:w] = jnp.zeros((1, w), F32)
            acc_sc[hh, :, 0:w] = jnp.zeros((HD_ATT, w), F32)

        def body(j, carry):
            for hh in heads:
                _attn_step(hh, pl.multiple_of(j * TK, TK), TK, q0, 0, w, False, *refs)
            return carry

        lax.fori_loop(0, n_unmasked, body, 0)
        for lo in range(0, w, TK_DIAG):
            for hh in heads:
                _attn_step(hh, q0 + lo, TK_DIAG, q0, lo, w - lo, True, *refs)
        for hh in heads:
            o_ref[pl.ds(q0, w), hh * HD_ATT:(hh + 1) * HD_ATT] = (
                acc_sc[hh, :, 0:w] * (1.0 / l_sc[hh, :, 0:w])).T

    def full_tile(i, carry):
        q_tile(pl.multiple_of(i * TQ, TQ), TQ, i * (TQ // TK))
        return carry

    lax.fori_loop(0, NQ_FULL, full_tile, 0)
    q_tile(NQ_FULL * TQ, TQ_LAST, NQ_FULL * TQ // TK)
    o_ref[ATT_ROWS:LP, :] = jnp.zeros((LP - ATT_ROWS, HEADS_PER_STEP * HD_ATT), F32)


def _attn_prompt(q_bf, k_bf, v_bf, fcs):
    blk = pl.BlockSpec((LP, HEADS_PER_STEP * HD_ATT), lambda b, h: (b, h))
    return pl.pallas_call(
        _attn_kernel,
        grid=(BATCH, H_ATT // HEADS_PER_STEP),
        in_specs=[blk, blk, blk, pl.BlockSpec((LP, SMALL_W), lambda b, h: (b, 0))],
        out_specs=blk,
        out_shape=jax.ShapeDtypeStruct((M_ROWS, D_ATT), F32),
        scratch_shapes=[pltpu.VMEM((HEADS_PER_STEP, LP, LANES), F32),
                        pltpu.VMEM((HEADS_PER_STEP, HD_ATT, LP), BF16),
                        pltpu.VMEM((HEADS_PER_STEP, 1, TQ_LAST), F32),
                        pltpu.VMEM((HEADS_PER_STEP, 1, TQ_LAST), F32),
                        pltpu.VMEM((HEADS_PER_STEP, HD_ATT, TQ_LAST), F32)],
        compiler_params=pltpu.CompilerParams(dimension_semantics=("arbitrary", "arbitrary"),
                                             vmem_limit_bytes=VMEM_LIMIT),
        name="attn_prompt",
    )(q_bf, k_bf, v_bf, fcs)


def _ssd_kernel(xbc_ref, sm_ref, z_ref, cw_ref, cb_ref, alog_ref, dsk_ref, g_ref, y_ref, st_ref,
                tail_sc, state_sc, y_sc, xd_sc):
    c = pl.program_id(1)
    nc = pl.num_programs(1)
    Q = Q_SSD

    @pl.when(c == 0)
    def _():
        tail_sc[...] = jnp.zeros_like(tail_sc)
        state_sc[...] = jnp.zeros_like(state_sc)

    x = xbc_ref[...]
    pt = tail_sc[...]
    row8 = lax.broadcasted_iota(jnp.int32, (SUBLANES, C_CONV), 0)
    acc = x * cw_ref[CONV_W - 1:CONV_W, :]
    for k in range(1, CONV_W):
        r = pltpu.roll(x, k, 0)
        first = jnp.where(row8 < k, pltpu.roll(pt, k, 0), r[0:SUBLANES])
        shifted = jnp.concatenate([first, r[SUBLANES:]], axis=0)
        acc = acc + shifted * cw_ref[CONV_W - 1 - k:CONV_W - k, :]
    tail_sc[...] = x[Q - SUBLANES:Q]
    xc = _silu(acc + cb_ref[...])
    xs = xc[:, :D_SSM]

    pos = c * Q + lax.broadcasted_iota(jnp.int32, (Q, LANES), 0)
    lane = lax.broadcasted_iota(jnp.int32, (Q, LANES), 1)
    is_dt = (pos < L_VALID) & (lane >= DT_LANE0) & (lane < DT_LANE0 + H_SSM)
    dt = jnp.where(is_dt, sm_ref[...], 0.0)
    a_dt = dt * (-jnp.exp(alog_ref[...]))
    acs = _cumsum_rows(a_dt)
    acs_t = acs.T
    last = acs[Q - 1:Q, :]
    decay_in = jnp.exp(last - acs)
    decay_out = jnp.exp(acs)
    chunk_decay = jnp.exp(last)
    tril = (lax.broadcasted_iota(jnp.int32, (Q, Q), 0) >= lax.broadcasted_iota(jnp.int32, (Q, Q), 1))

    for g in range(G_SSM):
        bg = xc[:, D_SSM + g * N_STATE:D_SSM + (g + 1) * N_STATE].astype(BF16)
        cg = xc[:, D_SSM + (G_SSM + g) * N_STATE:D_SSM + (G_SSM + g + 1) * N_STATE].astype(BF16)
        cb = _nt_dot(cg, bg)
        for r in range(R_SSM):
            hh = g * R_SSM + r
            col = DT_LANE0 + hh
            sl = slice(hh * HD_SSM, (hh + 1) * HD_SSM)
            xdt = xs[:, sl] * dt[:, col:col + 1]
            seg = jnp.exp(jnp.where(tril, acs[:, col:col + 1] - acs_t[col:col + 1, :], NEG_INF))
            y_diag = jnp.dot((cb * seg).astype(BF16), xdt.astype(BF16), preferred_element_type=F32)
            y_off = _nt_dot(cg, state_sc[hh].astype(BF16)) * decay_out[:, col:col + 1]
            y_sc[:, sl] = y_diag + y_off
            xd_sc[:, sl] = xdt * decay_in[:, col:col + 1]
    xd_t = xd_sc[...].T
    for g in range(G_SSM):
        bg = xc[:, D_SSM + g * N_STATE:D_SSM + (g + 1) * N_STATE].astype(BF16)
        for r in range(R_SSM):
            hh = g * R_SSM + r
            col = DT_LANE0 + hh
            new = jnp.dot(xd_t[hh * HD_SSM:(hh + 1) * HD_SSM, :].astype(BF16), bg,
                          preferred_element_type=F32)
            state_sc[hh] = state_sc[hh] * chunk_decay[:, col:col + 1] + new

    y = (y_sc[...] + dsk_ref[...] * xs) * _silu(z_ref[...])
    half = D_SSM // G_SSM
    parts = []
    for g in range(G_SSM):
        yg = y[:, g * half:(g + 1) * half]
        parts.append(yg * lax.rsqrt(jnp.mean(yg * yg, axis=-1, keepdims=True) + EPS))
    y_ref[...] = jnp.concatenate(parts, axis=1) * g_ref[...]

    @pl.when(c == nc - 1)
    def _():
        st_ref[0] = state_sc[...]


def _ssd_prompt(xbc, small, z, conv_w, conv_b, alog_row, dsk_row, g_row):
    nc = LP // Q_SSD
    row = lambda w_: pl.BlockSpec((Q_SSD, w_), lambda b, c: (b * nc + c, 0))
    full = lambda a: pl.BlockSpec(a.shape, lambda b, c: (0,) * a.ndim)
    return pl.pallas_call(
        _ssd_kernel,
        grid=(BATCH, nc),
        in_specs=[row(C_CONV), row(SMALL_W), row(D_SSM), full(conv_w), full(conv_b), full(alog_row),
                  full(dsk_row), full(g_row)],
        out_specs=(row(D_SSM),
                   pl.BlockSpec((1, H_SSM, HD_SSM, N_STATE), lambda b, c: (b, 0, 0, 0))),
        out_shape=(jax.ShapeDtypeStruct((M_ROWS, D_SSM), F32),
                   jax.ShapeDtypeStruct((BATCH, H_SSM, HD_SSM, N_STATE), F32)),
        scratch_shapes=[pltpu.VMEM((SUBLANES, C_CONV), F32),
                        pltpu.VMEM((H_SSM, HD_SSM, N_STATE), F32),
                        pltpu.VMEM((Q_SSD, D_SSM), F32),
                        pltpu.VMEM((Q_SSD, D_SSM), F32)],
        compiler_params=pltpu.CompilerParams(dimension_semantics=("arbitrary", "arbitrary"),
                                             vmem_limit_bytes=VMEM_LIMIT),
        name="ssd_prompt",
    )(xbc, small, z, conv_w, conv_b, alog_row, dsk_row, g_row)


def _outproj_kernel(att_ref, y_ref, atts_ref, ys_ref, x_ref, w_ref, ga_ref, gm_ref, gf_ref, x1_ref, h2_ref):
    r0 = SAMPLE_ROW0 % TM_PROJ
    rows = pl.program_id(0) * TM_PROJ + lax.broadcasted_iota(jnp.int32, (TM_PROJ, 1), 0)
    is_sample = (rows >= SAMPLE_ROW0) & (rows < SAMPLE_ROW0 + N_SAMPLE)

    def merged(main, small):
        placed = jnp.concatenate([jnp.zeros((r0, small.shape[1]), F32), small,
                                  jnp.zeros((TM_PROJ - r0 - N_SAMPLE, small.shape[1]), F32)], axis=0)
        return jnp.where(is_sample, placed, main)

    a = _rms(merged(att_ref[...], atts_ref[...]), ga_ref[...]).astype(BF16)
    y = merged(y_ref[...], ys_ref[...])
    mixed = jnp.dot(a, w_ref[0, 0:D_ATT, :], preferred_element_type=F32)
    mixed = mixed + jnp.dot(y.astype(BF16), w_ref[0, D_ATT:D_MIX, :], preferred_element_type=F32)
    x1 = x_ref[...] + _rms(mixed, gm_ref[...])
    x1_ref[...] = x1
    h2_ref[...] = _rms(x1, gf_ref[...]).astype(BF16)


def _outproj(layer, att, y, att_s, y_s, x, w, g_att, g_post_mix, g_pre_ffn):
    n = M_ROWS // TM_PROJ
    row = lambda w_: pl.BlockSpec((TM_PROJ, w_), lambda i: (i, 0))
    full = lambda a: pl.BlockSpec(a.shape, lambda i: (0,) * a.ndim)
    return pl.pallas_call(
        _outproj_kernel,
        grid=(n,),
        in_specs=[row(D_ATT), row(D_SSM), full(att_s), full(y_s), row(D_MODEL),
                  pl.BlockSpec((1,) + w.shape[1:], lambda i: (layer, 0, 0), pipeline_mode=pl.Buffered(1)),
                  full(g_att), full(g_post_mix), full(g_pre_ffn)],
        out_specs=(row(D_MODEL), row(D_MODEL)),
        out_shape=(jax.ShapeDtypeStruct((M_ROWS, D_MODEL), F32),
                   jax.ShapeDtypeStruct((M_ROWS, D_MODEL), BF16)),
        compiler_params=pltpu.CompilerParams(dimension_semantics=("arbitrary",),
                                             vmem_limit_bytes=VMEM_LIMIT),
        name="outproj",
    )(att, y, att_s, y_s, x, w, g_att, g_post_mix, g_pre_ffn)


def _ffn_kernel(h_ref, wg_ref, wu_ref, wd_ref, x1_ref, g_ref, o_ref, acc_sc):
    j = pl.program_id(1)

    @pl.when(j == 0)
    def _():
        acc_sc[...] = jnp.zeros_like(acc_sc)

    h = h_ref[...]
    gate = jnp.dot(h, wg_ref[0], preferred_element_type=F32)
    up = jnp.dot(h, wu_ref[0], preferred_element_type=F32)
    act = (_silu(gate) * up).astype(BF16)
    acc_sc[...] += jnp.dot(act, wd_ref[0], preferred_element_type=F32)

    @pl.when(j == pl.num_programs(1) - 1)
    def _():
        o_ref[...] = x1_ref[...] + _rms(acc_sc[...], g_ref[...])


def _ffn(layer, h2, wg, wu, wd, x1, g_post_ffn):
    return pl.pallas_call(
        _ffn_kernel,
        grid=(M_ROWS // TM_FFN, D_FF // TF_FFN),
        in_specs=[pl.BlockSpec((TM_FFN, D_MODEL), lambda i, j: (i, 0)),
                  pl.BlockSpec((1, D_MODEL, TF_FFN), lambda i, j: (layer, 0, j)),
                  pl.BlockSpec((1, D_MODEL, TF_FFN), lambda i, j: (layer, 0, j)),
                  pl.BlockSpec((1, TF_FFN, D_MODEL), lambda i, j: (layer, j, 0)),
                  pl.BlockSpec((TM_FFN, D_MODEL), lambda i, j: (i, 0)),
                  pl.BlockSpec(g_post_ffn.shape, lambda i, j: (0, 0))],
        out_specs=pl.BlockSpec((TM_FFN, D_MODEL), lambda i, j: (i, 0)),
        out_shape=jax.ShapeDtypeStruct((M_ROWS, D_MODEL), F32),
        scratch_shapes=[pltpu.VMEM((TM_FFN, D_MODEL), F32)],
        compiler_params=pltpu.CompilerParams(dimension_semantics=("arbitrary", "arbitrary"),
                                             vmem_limit_bytes=VMEM_LIMIT),
        name="ffn",
    )(h2, wg, wu, wd, x1, g_post_ffn)


NROW_DEC = DEC_SEQ * H_ATT
PAGE_ROWS = PAGE_SIZE * H_ATT
LF_ROWS = PAGE_ROWS // LANES
PAGES_PER_STEP = 8


def _dec_attn_kernel(pt_ref, q_ref, kn_ref, vn_ref, lfc_ref, lfr_ref, hmask_ref, *rest):
    del pt_ref
    n = PAGES_PER_STEP
    kp_refs, vp_refs, lfp_refs = rest[0:n], rest[n:2 * n], rest[2 * n:3 * n]
    o_ref, m_sc, l_sc, acc_sc, carry_sc, cn_sc = rest[3 * n:]
    j = pl.program_id(1)
    q = q_ref[0]

    def update(s, v_bf):
        m_prev = m_sc[...]
        m_new = jnp.maximum(m_prev, jnp.max(s, axis=1, keepdims=True))
        alpha = jnp.exp2(m_prev - m_new)
        p = jnp.exp2(s - m_new)
        l_sc[...] = alpha * l_sc[...] + jnp.sum(p, axis=1, keepdims=True)
        acc_sc[...] = alpha * acc_sc[...] + jnp.dot(p.astype(BF16), v_bf, preferred_element_type=F32)
        m_sc[...] = m_new

    @pl.when(j == 0)
    def _():
        r = lax.broadcasted_iota(jnp.int32, (NROW_DEC, NROW_DEC), 0)
        c = lax.broadcasted_iota(jnp.int32, (NROW_DEC, NROW_DEC), 1)
        same = (r & (H_ATT - 1)) == (c & (H_ATT - 1))
        keep = same & ((c >> 3) <= (r >> 3))
        cn_col = jnp.sum(jnp.where(keep, lfr_ref[0] * LOG2E, 0.0), axis=1, keepdims=True)
        cn_row = jnp.sum(jnp.where(same & ((r >> 3) <= (c >> 3)), lfc_ref[0] * LOG2E, 0.0),
                         axis=0, keepdims=True)
        m_sc[...] = jnp.full_like(m_sc, NEG_INF)
        l_sc[...] = jnp.zeros_like(l_sc)
        acc_sc[...] = jnp.zeros_like(acc_sc)
        carry_sc[...] = jnp.zeros_like(carry_sc)
        cn_sc[...] = cn_col
        s = _nt_dot(q, kn_ref[0].astype(BF16)) + cn_col - cn_row
        update(jnp.where(keep, s, NEG_INF), vn_ref[0].astype(BF16))

    lane = lax.broadcasted_iota(jnp.int32, (LF_ROWS, LANES), 1)
    rowi = lax.broadcasted_iota(jnp.int32, (LF_ROWS, LANES), 0)
    carry = carry_sc[...]
    base = cn_sc[...] + hmask_ref[...]
    scores = []
    for i in range(PAGES_PER_STEP):
        lf = lfp_refs[i][0] * LOG2E
        suf = lf
        tot = lf
        s_ = H_ATT
        while s_ < LANES:
            suf = suf + jnp.where(lane < LANES - s_, pltpu.roll(suf, LANES - s_, 1), 0.0)
            tot = tot + pltpu.roll(tot, s_, 1)
            s_ *= 2
        rsuf = tot
        s_ = 1
        while s_ < LF_ROWS:
            rsuf = rsuf + jnp.where(rowi < LF_ROWS - s_, pltpu.roll(rsuf, LF_ROWS - s_, 0), 0.0)
            s_ *= 2
        decay = carry + (suf - lf) + (rsuf - tot)
        carry = carry + rsuf[0:1, :]
        bias = jnp.concatenate([decay[r:r + 1, :] for r in range(LF_ROWS)], axis=1)
        k = kp_refs[i][0, 0].reshape(PAGE_ROWS, HD_ATT).astype(BF16)
        scores.append(_nt_dot(q, k) + (base + bias))
    carry_sc[...] = carry

    m_prev = m_sc[...]
    m_new = m_prev
    for s in scores:
        m_new = jnp.maximum(m_new, jnp.max(s, axis=1, keepdims=True))
    alpha = jnp.exp2(m_prev - m_new)
    l_new = alpha * l_sc[...]
    acc = alpha * acc_sc[...]
    for i, s in enumerate(scores):
        p = jnp.exp2(s - m_new)
        l_new = l_new + jnp.sum(p, axis=1, keepdims=True)
        v = vp_refs[i][0, 0].reshape(PAGE_ROWS, HD_ATT).astype(BF16)
        acc = acc + jnp.dot(p.astype(BF16), v, preferred_element_type=F32)
    l_sc[...] = l_new
    acc_sc[...] = acc
    m_sc[...] = m_new

    @pl.when(j == pl.num_programs(1) - 1)
    def _():
        o_ref[0] = acc_sc[...] * (1.0 / l_sc[...])


def _dec_attn(layer, page_flat, q_s, kn, vn, lf_col, lf_row, hmask, cache_k, cache_v, lfp):
    def page(i):
        return lambda b, j, pt: pt[b * N_PAGES + (N_PAGES - 1 - (j * PAGES_PER_STEP + i))]

    def kv_spec(i):
        pg = page(i)
        return pl.BlockSpec((1, 1, PAGE_SIZE, H_ATT, HD_ATT), lambda b, j, pt: (layer, pg(b, j, pt), 0, 0, 0))

    def lf_spec(i):
        pg = page(i)
        return pl.BlockSpec((1, LF_ROWS, LANES), lambda b, j, pt: (pg(b, j, pt), 0, 0))

    per_seq = lambda a: pl.BlockSpec((1,) + a.shape[1:], lambda b, j, pt: (b,) + (0,) * (a.ndim - 1))
    steps = range(PAGES_PER_STEP)
    grid_spec = pltpu.PrefetchScalarGridSpec(
        num_scalar_prefetch=1,
        grid=(DEC_BATCH, N_PAGES // PAGES_PER_STEP),
        in_specs=[per_seq(q_s), per_seq(kn), per_seq(vn), per_seq(lf_col), per_seq(lf_row),
                  pl.BlockSpec(hmask.shape, lambda b, j, pt: (0, 0))]
                 + [kv_spec(i) for i in steps] + [kv_spec(i) for i in steps] + [lf_spec(i) for i in steps],
        out_specs=pl.BlockSpec((1, NROW_DEC, HD_ATT), lambda b, j, pt: (b, 0, 0)),
        scratch_shapes=[pltpu.VMEM((NROW_DEC, 1), F32), pltpu.VMEM((NROW_DEC, 1), F32),
                        pltpu.VMEM((NROW_DEC, HD_ATT), F32), pltpu.VMEM((1, LANES), F32),
                        pltpu.VMEM((NROW_DEC, 1), F32)],
    )
    return pl.pallas_call(
        _dec_attn_kernel,
        grid_spec=grid_spec,
        out_shape=jax.ShapeDtypeStruct((DEC_BATCH, NROW_DEC, HD_ATT), F32),
        compiler_params=pltpu.CompilerParams(dimension_semantics=("arbitrary", "arbitrary"),
                                             vmem_limit_bytes=VMEM_LIMIT),
        name="dec_attn",
    )(page_flat, q_s, kn, vn, lf_col, lf_row, hmask, *([cache_k] * PAGES_PER_STEP),
      *([cache_v] * PAGES_PER_STEP), *([lfp] * PAGES_PER_STEP))


def _dec_ssm_kernel(xsh_ref, xsht_ref, cw_ref, cwt_ref, cb_ref, cbt_ref, dt_ref, alog_ref, zt_ref,
                    dskt_ref, gt_ref, st_ref, sto_ref, yt_ref, y_sc):
    xc = cb_ref[...]
    xct = cbt_ref[...]
    for j in range(CONV_W):
        xc = xc + xsh_ref[0, j] * cw_ref[j:j + 1, :]
        xct = xct + xsht_ref[0, j] * cwt_ref[:, j:j + 1]
    xc = _silu(xc)
    xct = _silu(xct)
    dt = dt_ref[0]
    da = jnp.exp(dt * (-jnp.exp(alog_ref[...])))
    for hh in range(H_SSM):
        g = hh // R_SSM
        s = st_ref[0, 0, hh]
        for t in range(DEC_SEQ):
            b = xc[t:t + 1, D_SSM + g * N_STATE:D_SSM + (g + 1) * N_STATE]
            c = xc[t:t + 1, D_SSM + (G_SSM + g) * N_STATE:D_SSM + (G_SSM + g + 1) * N_STATE]
            xcol = xct[hh * HD_SSM:(hh + 1) * HD_SSM, t:t + 1]
            s = s * da[t:t + 1, hh:hh + 1] + (xcol * dt[t:t + 1, hh:hh + 1]) * b
            y_sc[hh * HD_SSM:(hh + 1) * HD_SSM, t:t + 1] = jnp.sum(s * c, axis=1, keepdims=True)
        sto_ref[0, hh] = s
    y = (y_sc[...] + dskt_ref[...] * xct[0:D_SSM, :]) * _silu(zt_ref[0])
    half = D_SSM // G_SSM
    parts = []
    for g in range(G_SSM):
        yg = y[g * half:(g + 1) * half, :]
        parts.append(yg * lax.rsqrt(jnp.mean(yg * yg, axis=0, keepdims=True) + EPS))
    yt_ref[0] = jnp.concatenate(parts, axis=0) * gt_ref[...]


def _dec_ssm(layer, xsh, xsht, cw, cwt, cb, cbt, dt_s, alog, zt, dskt, gt, state_ssm):
    per_seq = lambda a: pl.BlockSpec((1,) + a.shape[1:], lambda b: (b,) + (0,) * (a.ndim - 1))
    full = lambda a: pl.BlockSpec(a.shape, lambda b: (0,) * a.ndim)
    return pl.pallas_call(
        _dec_ssm_kernel,
        grid=(DEC_BATCH,),
        in_specs=[per_seq(xsh), per_seq(xsht), full(cw), full(cwt), full(cb), full(cbt), per_seq(dt_s),
                  full(alog), per_seq(zt), full(dskt), full(gt),
                  pl.BlockSpec((1, 1, H_SSM, HD_SSM, N_STATE), lambda b: (layer, b, 0, 0, 0))],
        out_specs=(pl.BlockSpec((1, H_SSM, HD_SSM, N_STATE), lambda b: (b, 0, 0, 0)),
                   pl.BlockSpec((1, D_SSM, DEC_SEQ), lambda b: (b, 0, 0))),
        out_shape=(jax.ShapeDtypeStruct((DEC_BATCH, H_SSM, HD_SSM, N_STATE), F32),
                   jax.ShapeDtypeStruct((DEC_BATCH, D_SSM, DEC_SEQ), F32)),
        scratch_shapes=[pltpu.VMEM((D_SSM, DEC_SEQ), F32)],
        compiler_params=pltpu.CompilerParams(dimension_semantics=("arbitrary",),
                                             vmem_limit_bytes=VMEM_LIMIT),
        name="dec_ssm",
    )(xsh, xsht, cw, cwt, cb, cbt, dt_s, alog, zt, dskt, gt, state_ssm)


def _small_row(*pieces):
    v = jnp.concatenate([p.astype(F32) for p in pieces])
    return jnp.pad(v, (0, SMALL_W - v.shape[0]))[None, :]


def kernel(x_prompt, x_sample, cache_k, cache_v, cache_logf, state_ssm, state_conv, page_table, meta,
           w_in, b_f, dt_bias, a_log, d_skip, conv_w, conv_b, g_pre_mix, g_post_mix, g_att_out, g_ssm_out,
           w_out, g_pre_ffn, g_post_ffn, w_gate, w_up, w_down):
    assert x_prompt.shape == (BATCH, SEQ, D_MODEL) and x_sample.shape == (DEC_BATCH, DEC_SEQ, D_MODEL)
    assert page_table.shape == (DEC_BATCH, N_PAGES)
    n_pool = cache_k.shape[1]

    xp = jnp.concatenate([jnp.broadcast_to(meta.astype(F32)[None], (BATCH, N_META, D_MODEL)), x_prompt,
                          jnp.zeros((BATCH, LP - L_VALID, D_MODEL), F32)], axis=1)
    x = xp.reshape(M_ROWS, D_MODEL)
    x = lax.dynamic_update_slice(x, x_sample.reshape(N_SAMPLE, D_MODEL), (SAMPLE_ROW0, 0))

    page_flat = page_table.reshape(-1).astype(jnp.int32)
    lfp = cache_logf.reshape(DEPTH, n_pool, LF_ROWS, LANES)
    hmask = jnp.where((jnp.arange(NROW_DEC)[:, None] % H_ATT) == (jnp.arange(PAGE_ROWS)[None, :] % H_ATT),
                      0.0, NEG_INF).astype(F32)

    f0, f1 = 3 * D_ATT, 3 * D_ATT + H_ATT
    z0, z1 = f1, f1 + D_SSM
    c0, c1 = z1, z1 + C_CONV
    srows = slice(SAMPLE_ROW0, SAMPLE_ROW0 + N_SAMPLE)

    w_in_bf = jnp.concatenate(
        [w_in[:, :, :f0], w_in[:, :, z0:z1], w_in[:, :, c0:c1], w_in[:, :, f0:f1], w_in[:, :, c1:],
         jnp.zeros((DEPTH, D_MODEL, SMALL_W - H_ATT - H_SSM), w_in.dtype)], axis=2).astype(BF16)
    w_out_bf, w_gate_bf, w_up_bf, w_down_bf = (w.astype(BF16) for w in (w_out, w_gate, w_up, w_down))

    outs = {k: [] for k in ("k_p", "v_p", "lf_p", "k_s", "v_s", "lf_s", "ssm_p", "ssm_s", "conv_p", "conv_s")}
    for i in range(DEPTH):
        bias = _small_row(b_f[i], dt_bias[i])
        q_bf, k_bf, v_bf, k_p, v_p, k_s, v_s, z, xbc, small, fcs = _inproj(
            i, x, g_pre_mix[i][None, :], w_in_bf, bias)

        att = _attn_prompt(q_bf, k_bf, v_bf, fcs)
        alog_row = jnp.pad(a_log[i].astype(F32), (DT_LANE0, SMALL_W - DT_LANE0 - H_SSM))[None, :]
        dsk_row = jnp.repeat(d_skip[i].astype(F32), HD_SSM)[None, :]
        g_ssm_row = g_ssm_out[i].astype(F32)[None, :]
        y_ssm, st_p = _ssd_prompt(xbc, small, z, conv_w[i], conv_b[i][None, :], alog_row, dsk_row, g_ssm_row)

        q_s = q_bf[srows].reshape(DEC_BATCH, NROW_DEC, HD_ATT)
        kn = k_s.reshape(DEC_BATCH, NROW_DEC, HD_ATT)
        vn = v_s.reshape(DEC_BATCH, NROW_DEC, HD_ATT)
        lf_s = small[srows, :H_ATT].reshape(DEC_BATCH, DEC_SEQ, H_ATT)
        att_s = _dec_attn(i, page_flat, q_s, kn, vn, lf_s.reshape(DEC_BATCH, NROW_DEC, 1),
                          lf_s.reshape(DEC_BATCH, 1, NROW_DEC), hmask, cache_k, cache_v, lfp[i])
        xbc_s = xbc[srows].reshape(DEC_BATCH, DEC_SEQ, C_CONV)
        xpad = jnp.concatenate([state_conv[i].astype(F32), xbc_s], axis=1)
        xsh = jnp.stack([xpad[:, j:j + DEC_SEQ] for j in range(CONV_W)], axis=1)
        xsht = xsh.transpose(0, 1, 3, 2)
        dt_s = small[srows, DT_LANE0:DT_LANE0 + H_SSM].reshape(DEC_BATCH, DEC_SEQ, H_SSM)
        zt = z[srows].reshape(DEC_BATCH, DEC_SEQ, D_SSM).transpose(0, 2, 1)
        st_s, y_s_t = _dec_ssm(i, xsh, xsht, conv_w[i], conv_w[i].T, conv_b[i][None, :], conv_b[i][:, None],
                               dt_s, a_log[i].astype(F32)[None, :], zt, dsk_row.T, g_ssm_row.T, state_ssm)

        x1, h2 = _outproj(i, att, y_ssm, att_s.reshape(N_SAMPLE, D_ATT),
                          y_s_t.transpose(0, 2, 1).reshape(N_SAMPLE, D_SSM), x, w_out_bf,
                          g_att_out[i][None, :], g_post_mix[i][None, :], g_pre_ffn[i][None, :])
        x = _ffn(i, h2, w_gate_bf, w_up_bf, w_down_bf, x1, g_post_ffn[i][None, :])

        outs["k_p"].append(k_p.reshape(BATCH, L_VALID, H_ATT, HD_ATT))
        outs["v_p"].append(v_p.reshape(BATCH, L_VALID, H_ATT, HD_ATT))
        outs["lf_p"].append(small[:, :H_ATT].reshape(BATCH, LP, H_ATT)[:, :L_VALID])
        outs["k_s"].append(kn.reshape(DEC_BATCH, DEC_SEQ, H_ATT, HD_ATT))
        outs["v_s"].append(vn.reshape(DEC_BATCH, DEC_SEQ, H_ATT, HD_ATT))
        outs["lf_s"].append(lf_s)
        outs["ssm_p"].append(st_p)
        outs["ssm_s"].append(st_s)
        outs["conv_p"].append(xbc.reshape(BATCH, LP, C_CONV)[:, L_VALID - (CONV_W - 1):L_VALID])
        outs["conv_s"].append(xpad[:, DEC_SEQ:])

    xf = x.reshape(BATCH, LP, D_MODEL)
    y_prompt = xf[:, N_META:L_VALID]
    y_sample = x[srows].reshape(DEC_BATCH, DEC_SEQ, D_MODEL)
    st = lambda k: jnp.stack(outs[k])
    return (y_prompt, y_sample, st("k_p"), st("v_p"), st("lf_p"), st("k_s"), st("v_s"), st("lf_s"),
            st("ssm_p"), st("ssm_s"), st("conv_p"), st("conv_s"))
```

```python
import functools
import math

import jax
import jax.numpy as jnp
from jax import lax
from jax.experimental import pallas as pl
from jax.experimental.pallas import tpu as pltpu

D_MODEL = 2048
BATCH = 2
SEQ = 4096
DEPTH = 4
DEC_BATCH = 8
DEC_SEQ = 4
PAST_LEN = 16384
PAGE_SIZE = 128
N_META = 16
H_ATT = 8
HD_ATT = 128
D_ATT = H_ATT * HD_ATT
H_SSM = 16
HD_SSM = 64
D_SSM = H_SSM * HD_SSM
G_SSM = 2
R_SSM = H_SSM // G_SSM
N_STATE = 128
CONV_W = 4
C_CONV = D_SSM + 2 * G_SSM * N_STATE
D_MIX = D_ATT + D_SSM
D_FF = ((8 * D_MODEL + 3 * 256 - 1) // (3 * 256)) * 256
EPS = 1e-6
ATT_SCALE = HD_ATT ** -0.5

LANES = 128
SUBLANES = 8
L_VALID = N_META + SEQ
LP = 4352
M_ROWS = BATCH * LP
N_SAMPLE = DEC_BATCH * DEC_SEQ
SAMPLE_LOCAL = 4128
SAMPLE_ROW0 = (BATCH - 1) * LP + SAMPLE_LOCAL
N_PAGES = PAST_LEN // PAGE_SIZE
SMALL_W = LANES
D_IN_PACKED = 3 * D_ATT + D_SSM + C_CONV + SMALL_W
COL_Z = 3 * D_ATT
COL_XBC = COL_Z + D_SSM
COL_SMALL = COL_XBC + C_CONV
DT_LANE0 = H_ATT

TM_PROJ = 256
TM_FFN = 512
TF_FFN = 512
TQ = 512
TK = 512
TK_DIAG = 256
HEADS_PER_STEP = 2
NQ_FULL = L_VALID // TQ - 1
TQ_LAST = -(-(L_VALID - NQ_FULL * TQ) // LANES) * LANES
ATT_ROWS = NQ_FULL * TQ + TQ_LAST
LOG2E = math.log2(math.e)
Q_SSD = 128
VMEM_LIMIT = 56 * 1024 * 1024

F32 = jnp.float32
BF16 = jnp.bfloat16
NEG_INF = float("-inf")


def _nt_dot(a, b):
    return lax.dot_general(a, b, (((1,), (1,)), ((), ())), preferred_element_type=F32)


def _silu(x):
    return x * (1.0 / (1.0 + jnp.exp(-x)))


def _rms(x, g):
    return x * lax.rsqrt(jnp.mean(x * x, axis=-1, keepdims=True) + EPS) * g


def _cumsum_rows(x):
    n = x.shape[0]
    row = lax.broadcasted_iota(jnp.int32, x.shape, 0)
    s = 1
    while s < n:
        x = x + jnp.where(row >= s, pltpu.roll(x, s, 0), 0.0)
        s *= 2
    return x


def _inproj_kernel(x_ref, g_ref, w_ref, b_ref, q_ref, kb_ref, vt_ref, kp_ref, vp_ref, ks_ref, vs_ref,
                   z_ref, xbc_ref, sm_ref, fcs_ref, carry_sc):
    i = pl.program_id(0)
    h = _rms(x_ref[...], g_ref[...]).astype(BF16)

    def mm(c0, c1):
        return jnp.dot(h, w_ref[0, :, c0:c1], preferred_element_type=F32)

    q_ref[...] = ((mm(0, D_ATT) * ATT_SCALE) * LOG2E).astype(BF16)
    k = mm(D_ATT, 2 * D_ATT)
    kp_ref[0] = k
    kb_ref[...] = k.astype(BF16)
    v = mm(2 * D_ATT, 3 * D_ATT)
    vp_ref[0] = v
    vt_ref[...] = v.T.astype(BF16)

    @pl.when(i == SAMPLE_ROW0 // TM_PROJ)
    def _():
        r0 = SAMPLE_ROW0 % TM_PROJ
        ks_ref[...] = k[r0:r0 + N_SAMPLE]
        vs_ref[...] = v[r0:r0 + N_SAMPLE]

    z_ref[...] = mm(COL_Z, COL_XBC)
    xbc_ref[...] = mm(COL_XBC, COL_SMALL)
    u = mm(COL_SMALL, D_IN_PACKED) + b_ref[...]
    t = jnp.log(1.0 + jnp.exp(-jnp.abs(u)))
    lane = lax.broadcasted_iota(jnp.int32, u.shape, 1)
    sm = jnp.where(lane < H_ATT, jnp.minimum(u, 0.0) - t, jnp.maximum(u, 0.0) + t)
    sm_ref[...] = sm

    @pl.when(i % (LP // TM_PROJ) == 0)
    def _():
        carry_sc[...] = jnp.zeros_like(carry_sc)

    cs = _cumsum_rows(sm) + carry_sc[...]
    fcs_ref[...] = cs
    carry_sc[...] = cs[TM_PROJ - 1:TM_PROJ, :]


def _inproj(layer, x, g, w, bias):
    n = M_ROWS // TM_PROJ
    row = lambda w_: pl.BlockSpec((TM_PROJ, w_), lambda i: (i, 0))
    full = lambda a: pl.BlockSpec(a.shape, lambda i: (0,) * a.ndim)
    tiles_per_seq = LP // TM_PROJ
    cache = pl.BlockSpec((1, TM_PROJ, D_ATT), lambda i: (i // tiles_per_seq, i % tiles_per_seq, 0))
    sample = pl.BlockSpec((N_SAMPLE, D_ATT), lambda i: (0, 0))
    outs = (
        (jax.ShapeDtypeStruct((M_ROWS, D_ATT), BF16), row(D_ATT)),
        (jax.ShapeDtypeStruct((M_ROWS, D_ATT), BF16), row(D_ATT)),
        (jax.ShapeDtypeStruct((D_ATT, M_ROWS), BF16),
         pl.BlockSpec((D_ATT, TM_PROJ), lambda i: (0, i))),
        (jax.ShapeDtypeStruct((BATCH, L_VALID, D_ATT), F32), cache),
        (jax.ShapeDtypeStruct((BATCH, L_VALID, D_ATT), F32), cache),
        (jax.ShapeDtypeStruct((N_SAMPLE, D_ATT), F32), sample),
        (jax.ShapeDtypeStruct((N_SAMPLE, D_ATT), F32), sample),
        (jax.ShapeDtypeStruct((M_ROWS, D_SSM), F32), row(D_SSM)),
        (jax.ShapeDtypeStruct((M_ROWS, C_CONV), F32), row(C_CONV)),
        (jax.ShapeDtypeStruct((M_ROWS, SMALL_W), F32), row(SMALL_W)),
        (jax.ShapeDtypeStruct((M_ROWS, SMALL_W), F32), row(SMALL_W)),
    )
    return pl.pallas_call(
        _inproj_kernel,
        grid=(n,),
        in_specs=[row(D_MODEL), full(g),
                  pl.BlockSpec((1,) + w.shape[1:], lambda i: (layer, 0, 0), pipeline_mode=pl.Buffered(1)),
                  full(bias)],
        out_specs=tuple(o[1] for o in outs),
        out_shape=tuple(o[0] for o in outs),
        scratch_shapes=[pltpu.VMEM((1, SMALL_W), F32)],
        compiler_params=pltpu.CompilerParams(dimension_semantics=("arbitrary",),
                                             vmem_limit_bytes=VMEM_LIMIT),
        name="inproj",
    )(x, g, w, bias)


def _attn_step(hh, ks, tk, q0, lo, w, masked, q_ref, k_ref, fkb_sc, vt_ref, m_sc, l_sc, acc_sc):
    s = _attn_scores(hh, ks, tk, q0 + lo, w, q_ref, k_ref)
    _attn_accumulate(hh, s, ks, tk, lo, w, masked, fkb_sc, vt_ref, m_sc, l_sc, acc_sc)


def _attn_scores(hh, ks, tk, qs, w, q_ref, k_ref):
    hs = slice(hh * HD_ATT, (hh + 1) * HD_ATT)
    return _nt_dot(k_ref[pl.ds(ks, tk), hs], q_ref[pl.ds(qs, w), hs])


def _attn_accumulate(hh, s, ks, tk, lo, w, masked, fkb_sc, vt_ref, m_sc, l_sc, acc_sc):
    s = s - jnp.concatenate([fkb_sc[hh, pl.ds(ks, tk), :]] * (w // LANES), axis=1)
    if masked:
        s = jnp.where(lax.broadcasted_iota(jnp.int32, (tk, w), 0) <= lax.broadcasted_iota(jnp.int32, (tk, w), 1),
                      s, NEG_INF)
    m_prev = m_sc[hh, :, lo:lo + w]
    m_new = jnp.maximum(m_prev, jnp.max(s, axis=0, keepdims=True))
    alpha = jnp.exp2(m_prev - m_new)
    p = jnp.exp2(s - m_new)
    l_sc[hh, :, lo:lo + w] = alpha * l_sc[hh, :, lo:lo + w] + jnp.sum(p, axis=0, keepdims=True)
    acc_sc[hh, :, lo:lo + w] = alpha * acc_sc[hh, :, lo:lo + w] + jnp.dot(
        vt_ref[hh * HD_ATT:(hh + 1) * HD_ATT, pl.ds(ks, tk)], p.astype(BF16), preferred_element_type=F32)
    m_sc[hh, :, lo:lo + w] = m_new


def _attn_kernel(q_ref, k_ref, vt_ref, fcs_ref, o_ref, fkb_sc, s_sc, m_sc, l_sc, acc_sc):
    hp = pl.program_id(1)
    lane = lax.broadcasted_iota(jnp.int32, (TK_DIAG, LANES), 1)
    heads = range(HEADS_PER_STEP)

    def prep(c, carry):
        r0 = pl.multiple_of(c * TK_DIAG, TK_DIAG)
        f = fcs_ref[pl.ds(r0, TK_DIAG), :]
        for hh in heads:
            col = jnp.sum(jnp.where(lane == hp * HEADS_PER_STEP + hh, f, 0.0), axis=1, keepdims=True) * LOG2E
            fkb_sc[hh, pl.ds(r0, TK_DIAG), :] = jnp.broadcast_to(col, (TK_DIAG, LANES))
        return carry

    lax.fori_loop(0, LP // TK_DIAG, prep, 0)
    acc_refs = (fkb_sc, vt_ref, m_sc, l_sc, acc_sc)

    def q_tile(q0, w, n_unmasked):
        n = jnp.asarray(n_unmasked, jnp.int32)
        for hh in heads:
            m_sc[hh, :, 0:w] = jnp.full((1, w), NEG_INF, F32)
            l_sc[hh, :, 0:w] = jnp.zeros((1, w), F32)
            acc_sc[hh, :, 0:w] = jnp.zeros((HD_ATT, w), F32)

        def scores_to(slot, j):
            for hh in heads:
                s_sc[slot, hh, :, 0:w] = _attn_scores(hh, pl.multiple_of(j * TK, TK), TK, q0, w, q_ref, k_ref)

        def accumulate_from(slot, j):
            for hh in heads:
                _attn_accumulate(hh, s_sc[slot, hh, :, 0:w], pl.multiple_of(j * TK, TK), TK, 0, w, False,
                                 *acc_refs)

        @pl.when(n > 0)
        def _():
            scores_to(0, 0)

        def body(jj, carry):
            scores_to(1, 2 * jj + 1)
            accumulate_from(0, 2 * jj)
            scores_to(0, 2 * jj + 2)
            accumulate_from(1, 2 * jj + 1)
            return carry

        n_pairs = jnp.maximum(n - 1, 0) // 2
        lax.fori_loop(0, n_pairs, body, 0)

        @pl.when((n > 0) & ((n & 1) == 1))
        def _():
            accumulate_from(0, 2 * n_pairs)

        @pl.when((n > 0) & ((n & 1) == 0))
        def _():
            scores_to(1, 2 * n_pairs + 1)
            accumulate_from(0, 2 * n_pairs)
            accumulate_from(1, 2 * n_pairs + 1)

        for lo in range(0, w, TK_DIAG):
            for hh in heads:
                _attn_step(hh, q0 + lo, TK_DIAG, q0, lo, w - lo, True, q_ref, k_ref, *acc_refs)
        for hh in heads:
            o_ref[pl.ds(q0, w), hh * HD_ATT:(hh + 1) * HD_ATT] = (
                acc_sc[hh, :, 0:w] * (1.0 / l_sc[hh, :, 0:w])).T

    def full_tile(i, carry):
        q_tile(pl.multiple_of(i * TQ, TQ), TQ, i * (TQ // TK))
        return carry

    lax.fori_loop(0, NQ_FULL, full_tile, 0)
    q_tile(NQ_FULL * TQ, TQ_LAST, NQ_FULL * TQ // TK)
    o_ref[ATT_ROWS:LP, :] = jnp.zeros((LP - ATT_ROWS, HEADS_PER_STEP * HD_ATT), F32)


def _attn_prompt(q_bf, k_bf, vt_bf, fcs):
    blk = pl.BlockSpec((LP, HEADS_PER_STEP * HD_ATT), lambda b, h: (b, h))
    return pl.pallas_call(
        _attn_kernel,
        grid=(BATCH, H_ATT // HEADS_PER_STEP),
        in_specs=[blk, blk, pl.BlockSpec((HEADS_PER_STEP * HD_ATT, LP), lambda b, h: (h, b)),
                  pl.BlockSpec((LP, SMALL_W), lambda b, h: (b, 0))],
        out_specs=blk,
        out_shape=jax.ShapeDtypeStruct((M_ROWS, D_ATT), F32),
        scratch_shapes=[pltpu.VMEM((HEADS_PER_STEP, LP, LANES), F32),
                        pltpu.VMEM((2, HEADS_PER_STEP, TK, TQ_LAST), F32),
                        pltpu.VMEM((HEADS_PER_STEP, 1, TQ_LAST), F32),
                        pltpu.VMEM((HEADS_PER_STEP, 1, TQ_LAST), F32),
                        pltpu.VMEM((HEADS_PER_STEP, HD_ATT, TQ_LAST), F32)],
        compiler_params=pltpu.CompilerParams(dimension_semantics=("arbitrary", "arbitrary"),
                                             vmem_limit_bytes=VMEM_LIMIT),
        name="attn_prompt",
    )(q_bf, k_bf, vt_bf, fcs)


PAIR_W = 2 * HD_SSM


def _expand(v, e_ref):
    hi = v.astype(BF16)
    r1 = v - hi.astype(F32)
    mid = r1.astype(BF16)
    lo = (r1 - mid.astype(F32)).astype(BF16)
    e = e_ref[...]
    return (jnp.dot(hi, e, preferred_element_type=F32) + jnp.dot(mid, e, preferred_element_type=F32)) + jnp.dot(
        lo, e, preferred_element_type=F32)


def _head_expansion(width):
    r = jnp.arange(SMALL_W)[:, None]
    c = jnp.arange(H_SSM * width)[None, :]
    return (r == DT_LANE0 + c // width).astype(BF16)


def _ssd_kernel(xbc_ref, sm_ref, z_ref, cw_ref, cb_ref, alog_ref, dsk_ref, g_ref, e_ref, e2_ref, y_ref, st_ref,
                tail_sc, state_sc):
    c = pl.program_id(1)
    nc = pl.num_programs(1)
    Q = Q_SSD

    @pl.when(c == 0)
    def _():
        tail_sc[...] = jnp.zeros_like(tail_sc)
        state_sc[...] = jnp.zeros_like(state_sc)

    x = xbc_ref[...]
    pt = tail_sc[...]
    row8 = lax.broadcasted_iota(jnp.int32, (SUBLANES, C_CONV), 0)
    acc = x * cw_ref[CONV_W - 1:CONV_W, :]
    for k in range(1, CONV_W):
        r = pltpu.roll(x, k, 0)
        first = jnp.where(row8 < k, pltpu.roll(pt, k, 0), r[0:SUBLANES])
        shifted = jnp.concatenate([first, r[SUBLANES:]], axis=0)
        acc = acc + shifted * cw_ref[CONV_W - 1 - k:CONV_W - k, :]
    tail_sc[...] = x[Q - SUBLANES:Q]
    xc = _silu(acc + cb_ref[...])
    xs = xc[:, :D_SSM]

    pos = c * Q + lax.broadcasted_iota(jnp.int32, (Q, LANES), 0)
    lane = lax.broadcasted_iota(jnp.int32, (Q, LANES), 1)
    is_dt = (pos < L_VALID) & (lane >= DT_LANE0) & (lane < DT_LANE0 + H_SSM)
    dt = jnp.where(is_dt, sm_ref[...], 0.0)
    a_dt = dt * (-jnp.exp(alog_ref[...]))
    acs = _cumsum_rows(a_dt)
    acs_t = acs.T
    last = acs[Q - 1:Q, :]
    chunk_decay = jnp.exp(last)
    x_dt = xs * _expand(dt, e_ref)
    xd_t = (x_dt * _expand(jnp.exp(last - acs), e_ref)).T
    dout_x = _expand(jnp.exp(acs), e_ref)
    acs_x = _expand(acs, e2_ref)
    tril = (lax.broadcasted_iota(jnp.int32, (Q, Q), 0) >= lax.broadcasted_iota(jnp.int32, (Q, Q), 1))
    low = lane < HD_SSM
    rowi = lax.broadcasted_iota(jnp.int32, (PAIR_W, N_STATE), 0)

    ys = []
    for g in range(G_SSM):
        bg = xc[:, D_SSM + g * N_STATE:D_SSM + (g + 1) * N_STATE].astype(BF16)
        cg = xc[:, D_SSM + (G_SSM + g) * N_STATE:D_SSM + (G_SSM + g + 1) * N_STATE].astype(BF16)
        cb = _nt_dot(cg, bg)
        for pr in range(R_SSM // 2):
            h0 = g * R_SSM + 2 * pr
            pi = h0 // 2
            ps = slice(pi * PAIR_W, (pi + 1) * PAIR_W)
            ms = []
            for hh in (h0, h0 + 1):
                col = DT_LANE0 + hh
                seg = jnp.exp(jnp.where(tril, acs_x[:, hh * Q:(hh + 1) * Q] - acs_t[col:col + 1, :], NEG_INF))
                ms.append((cb * seg).astype(BF16))
            xp = x_dt[:, ps]
            rhs = jnp.concatenate([jnp.where(low, xp, 0.0), jnp.where(low, 0.0, xp)], axis=0).astype(BF16)
            y_diag = jnp.dot(jnp.concatenate(ms, axis=1), rhs, preferred_element_type=F32)
            st = state_sc[pi]
            y_off = _nt_dot(cg, st.astype(BF16)) * dout_x[:, ps]
            ys.append(y_diag + y_off)
            new = jnp.dot(xd_t[ps, :].astype(BF16), bg, preferred_element_type=F32)
            c0 = DT_LANE0 + h0
            cd = jnp.where(rowi < HD_SSM, chunk_decay[:, c0:c0 + 1], chunk_decay[:, c0 + 1:c0 + 2])
            state_sc[pi] = st * cd + new

    y = (jnp.concatenate(ys, axis=1) + dsk_ref[...] * xs) * _silu(z_ref[...])
    half = D_SSM // G_SSM
    parts = []
    for g in range(G_SSM):
        yg = y[:, g * half:(g + 1) * half]
        parts.append(yg * lax.rsqrt(jnp.mean(yg * yg, axis=-1, keepdims=True) + EPS))
    y_ref[...] = jnp.concatenate(parts, axis=1) * g_ref[...]

    @pl.when(c == nc - 1)
    def _():
        st_ref[0] = state_sc[...].reshape(H_SSM, HD_SSM, N_STATE)


def _ssd_prompt(xbc, small, z, conv_w, conv_b, alog_row, dsk_row, g_row, e1, e2):
    nc = LP // Q_SSD
    row = lambda w_: pl.BlockSpec((Q_SSD, w_), lambda b, c: (b * nc + c, 0))
    full = lambda a: pl.BlockSpec(a.shape, lambda b, c: (0,) * a.ndim)
    return pl.pallas_call(
        _ssd_kernel,
        grid=(BATCH, nc),
        in_specs=[row(C_CONV), row(SMALL_W), row(D_SSM), full(conv_w), full(conv_b), full(alog_row),
                  full(dsk_row), full(g_row), full(e1), full(e2)],
        out_specs=(row(D_SSM),
                   pl.BlockSpec((1, H_SSM, HD_SSM, N_STATE), lambda b, c: (b, 0, 0, 0))),
        out_shape=(jax.ShapeDtypeStruct((M_ROWS, D_SSM), F32),
                   jax.ShapeDtypeStruct((BATCH, H_SSM, HD_SSM, N_STATE), F32)),
        scratch_shapes=[pltpu.VMEM((SUBLANES, C_CONV), F32),
                        pltpu.VMEM((H_SSM // 2, PAIR_W, N_STATE), F32)],
        compiler_params=pltpu.CompilerParams(dimension_semantics=("arbitrary", "arbitrary"),
                                             vmem_limit_bytes=VMEM_LIMIT),
        name="ssd_prompt",
    )(xbc, small, z, conv_w, conv_b, alog_row, dsk_row, g_row, e1, e2)


def _outproj_kernel(att_ref, y_ref, atts_ref, ys_ref, x_ref, w_ref, ga_ref, gm_ref, gf_ref, x1_ref, h2_ref):
    r0 = SAMPLE_ROW0 % TM_PROJ
    rows = pl.program_id(0) * TM_PROJ + lax.broadcasted_iota(jnp.int32, (TM_PROJ, 1), 0)
    is_sample = (rows >= SAMPLE_ROW0) & (rows < SAMPLE_ROW0 + N_SAMPLE)

    def merged(main, small):
        placed = jnp.concatenate([jnp.zeros((r0, small.shape[1]), F32), small,
                                  jnp.zeros((TM_PROJ - r0 - N_SAMPLE, small.shape[1]), F32)], axis=0)
        return jnp.where(is_sample, placed, main)

    a = _rms(merged(att_ref[...], atts_ref[...]), ga_ref[...]).astype(BF16)
    y = merged(y_ref[...], ys_ref[...])
    mixed = jnp.dot(a, w_ref[0, 0:D_ATT, :], preferred_element_type=F32)
    mixed = mixed + jnp.dot(y.astype(BF16), w_ref[0, D_ATT:D_MIX, :], preferred_element_type=F32)
    x1 = x_ref[...] + _rms(mixed, gm_ref[...])
    x1_ref[...] = x1
    h2_ref[...] = _rms(x1, gf_ref[...]).astype(BF16)


def _outproj(layer, att, y, att_s, y_s, x, w, g_att, g_post_mix, g_pre_ffn):
    n = M_ROWS // TM_PROJ
    row = lambda w_: pl.BlockSpec((TM_PROJ, w_), lambda i: (i, 0))
    full = lambda a: pl.BlockSpec(a.shape, lambda i: (0,) * a.ndim)
    return pl.pallas_call(
        _outproj_kernel,
        grid=(n,),
        in_specs=[row(D_ATT), row(D_SSM), full(att_s), full(y_s), row(D_MODEL),
                  pl.BlockSpec((1,) + w.shape[1:], lambda i: (layer, 0, 0), pipeline_mode=pl.Buffered(1)),
                  full(g_att), full(g_post_mix), full(g_pre_ffn)],
        out_specs=(row(D_MODEL), row(D_MODEL)),
        out_shape=(jax.ShapeDtypeStruct((M_ROWS, D_MODEL), F32),
                   jax.ShapeDtypeStruct((M_ROWS, D_MODEL), BF16)),
        compiler_params=pltpu.CompilerParams(dimension_semantics=("arbitrary",),
                                             vmem_limit_bytes=VMEM_LIMIT),
        name="outproj",
    )(att, y, att_s, y_s, x, w, g_att, g_post_mix, g_pre_ffn)


def _ffn_kernel(h_ref, wg_ref, wu_ref, wd_ref, x1_ref, g_ref, o_ref):
    j = pl.program_id(1)

    @pl.when(j == 0)
    def _():
        o_ref[...] = jnp.zeros_like(o_ref)

    h = h_ref[...]
    gate = jnp.dot(h, wg_ref[0], preferred_element_type=F32)
    up = jnp.dot(h, wu_ref[0], preferred_element_type=F32)
    act = (_silu(gate) * up).astype(BF16)
    o_ref[...] += jnp.dot(act, wd_ref[0], preferred_element_type=F32)

    @pl.when(j == pl.num_programs(1) - 1)
    def _():
        o_ref[...] = x1_ref[...] + _rms(o_ref[...], g_ref[...])


def _ffn(layer, h2, wg, wu, wd, x1, g_post_ffn):
    return pl.pallas_call(
        _ffn_kernel,
        grid=(M_ROWS // TM_FFN, D_FF // TF_FFN),
        in_specs=[pl.BlockSpec((TM_FFN, D_MODEL), lambda i, j: (i, 0)),
                  pl.BlockSpec((1, D_MODEL, TF_FFN), lambda i, j: (layer, 0, j)),
                  pl.BlockSpec((1, D_MODEL, TF_FFN), lambda i, j: (layer, 0, j)),
                  pl.BlockSpec((1, TF_FFN, D_MODEL), lambda i, j: (layer, j, 0)),
                  pl.BlockSpec((TM_FFN, D_MODEL), lambda i, j: (i, 0)),
                  pl.BlockSpec(g_post_ffn.shape, lambda i, j: (0, 0))],
        out_specs=pl.BlockSpec((TM_FFN, D_MODEL), lambda i, j: (i, 0)),
        out_shape=jax.ShapeDtypeStruct((M_ROWS, D_MODEL), F32),
        compiler_params=pltpu.CompilerParams(dimension_semantics=("arbitrary", "arbitrary"),
                                             vmem_limit_bytes=VMEM_LIMIT),
        name="ffn",
    )(h2, wg, wu, wd, x1, g_post_ffn)


NROW_DEC = DEC_SEQ * H_ATT
PAGE_ROWS = PAGE_SIZE * H_ATT
LF_ROWS = PAGE_ROWS // LANES
PAGES_PER_STEP = 8


def _dec_attn_kernel(pt_ref, q_ref, kn_ref, vn_ref, lfc_ref, lfr_ref, hmask_ref, *rest):
    del pt_ref
    n = PAGES_PER_STEP
    kp_refs, vp_refs, lfp_refs = rest[0:n], rest[n:2 * n], rest[2 * n:3 * n]
    o_ref, m_sc, l_sc, acc_sc, carry_sc, cn_sc = rest[3 * n:]
    j = pl.program_id(1)
    q = q_ref[0]

    def update(s, v_bf):
        m_prev = m_sc[...]
        m_new = jnp.maximum(m_prev, jnp.max(s, axis=1, keepdims=True))
        alpha = jnp.exp2(m_prev - m_new)
        p = jnp.exp2(s - m_new)
        l_sc[...] = alpha * l_sc[...] + jnp.sum(p, axis=1, keepdims=True)
        acc_sc[...] = alpha * acc_sc[...] + jnp.dot(p.astype(BF16), v_bf, preferred_element_type=F32)
        m_sc[...] = m_new

    @pl.when(j == 0)
    def _():
        r = lax.broadcasted_iota(jnp.int32, (NROW_DEC, NROW_DEC), 0)
        c = lax.broadcasted_iota(jnp.int32, (NROW_DEC, NROW_DEC), 1)
        same = (r & (H_ATT - 1)) == (c & (H_ATT - 1))
        keep = same & ((c >> 3) <= (r >> 3))
        cn_col = jnp.sum(jnp.where(keep, lfr_ref[0] * LOG2E, 0.0), axis=1, keepdims=True)
        cn_row = jnp.sum(jnp.where(same & ((r >> 3) <= (c >> 3)), lfc_ref[0] * LOG2E, 0.0),
                         axis=0, keepdims=True)
        m_sc[...] = jnp.full_like(m_sc, NEG_INF)
        l_sc[...] = jnp.zeros_like(l_sc)
        acc_sc[...] = jnp.zeros_like(acc_sc)
        carry_sc[...] = jnp.zeros_like(carry_sc)
        cn_sc[...] = cn_col
        s = _nt_dot(q, kn_ref[0].astype(BF16)) + cn_col - cn_row
        update(jnp.where(keep, s, NEG_INF), vn_ref[0].astype(BF16))

    lane = lax.broadcasted_iota(jnp.int32, (LF_ROWS, LANES), 1)
    rowi = lax.broadcasted_iota(jnp.int32, (LF_ROWS, LANES), 0)
    carry = carry_sc[...]
    base = cn_sc[...] + hmask_ref[...]
    scores = []
    for i in range(PAGES_PER_STEP):
        lf = lfp_refs[i][0] * LOG2E
        suf = lf
        tot = lf
        s_ = H_ATT
        while s_ < LANES:
            suf = suf + jnp.where(lane < LANES - s_, pltpu.roll(suf, LANES - s_, 1), 0.0)
            tot = tot + pltpu.roll(tot, s_, 1)
            s_ *= 2
        rsuf = tot
        s_ = 1
        while s_ < LF_ROWS:
            rsuf = rsuf + jnp.where(rowi < LF_ROWS - s_, pltpu.roll(rsuf, LF_ROWS - s_, 0), 0.0)
            s_ *= 2
        decay = carry + (suf - lf) + (rsuf - tot)
        carry = carry + rsuf[0:1, :]
        bias = jnp.concatenate([decay[r:r + 1, :] for r in range(LF_ROWS)], axis=1)
        k = kp_refs[i][0, 0].reshape(PAGE_ROWS, HD_ATT).astype(BF16)
        scores.append(_nt_dot(q, k) + (base + bias))
    carry_sc[...] = carry

    m_prev = m_sc[...]
    m_new = m_prev
    for s in scores:
        m_new = jnp.maximum(m_new, jnp.max(s, axis=1, keepdims=True))
    alpha = jnp.exp2(m_prev - m_new)
    l_new = alpha * l_sc[...]
    acc = alpha * acc_sc[...]
    for i, s in enumerate(scores):
        p = jnp.exp2(s - m_new)
        l_new = l_new + jnp.sum(p, axis=1, keepdims=True)
        v = vp_refs[i][0, 0].reshape(PAGE_ROWS, HD_ATT).astype(BF16)
        acc = acc + jnp.dot(p.astype(BF16), v, preferred_element_type=F32)
    l_sc[...] = l_new
    acc_sc[...] = acc
    m_sc[...] = m_new

    @pl.when(j == pl.num_programs(1) - 1)
    def _():
        o_ref[0] = acc_sc[...] * (1.0 / l_sc[...])


def _dec_attn(layer, page_flat, q_s, kn, vn, lf_col, lf_row, hmask, cache_k, cache_v, lfp):
    def page(i):
        return lambda b, j, pt: pt[b * N_PAGES + (N_PAGES - 1 - (j * PAGES_PER_STEP + i))]

    def kv_spec(i):
        pg = page(i)
        return pl.BlockSpec((1, 1, PAGE_SIZE, H_ATT, HD_ATT), lambda b, j, pt: (layer, pg(b, j, pt), 0, 0, 0))

    def lf_spec(i):
        pg = page(i)
        return pl.BlockSpec((1, LF_ROWS, LANES), lambda b, j, pt: (pg(b, j, pt), 0, 0))

    per_seq = lambda a: pl.BlockSpec((1,) + a.shape[1:], lambda b, j, pt: (b,) + (0,) * (a.ndim - 1))
    steps = range(PAGES_PER_STEP)
    grid_spec = pltpu.PrefetchScalarGridSpec(
        num_scalar_prefetch=1,
        grid=(DEC_BATCH, N_PAGES // PAGES_PER_STEP),
        in_specs=[per_seq(q_s), per_seq(kn), per_seq(vn), per_seq(lf_col), per_seq(lf_row),
                  pl.BlockSpec(hmask.shape, lambda b, j, pt: (0, 0))]
                 + [kv_spec(i) for i in steps] + [kv_spec(i) for i in steps] + [lf_spec(i) for i in steps],
        out_specs=pl.BlockSpec((1, NROW_DEC, HD_ATT), lambda b, j, pt: (b, 0, 0)),
        scratch_shapes=[pltpu.VMEM((NROW_DEC, 1), F32), pltpu.VMEM((NROW_DEC, 1), F32),
                        pltpu.VMEM((NROW_DEC, HD_ATT), F32), pltpu.VMEM((1, LANES), F32),
                        pltpu.VMEM((NROW_DEC, 1), F32)],
    )
    return pl.pallas_call(
        _dec_attn_kernel,
        grid_spec=grid_spec,
        out_shape=jax.ShapeDtypeStruct((DEC_BATCH, NROW_DEC, HD_ATT), F32),
        compiler_params=pltpu.CompilerParams(dimension_semantics=("arbitrary", "arbitrary"),
                                             vmem_limit_bytes=VMEM_LIMIT),
        name="dec_attn",
    )(page_flat, q_s, kn, vn, lf_col, lf_row, hmask, *([cache_k] * PAGES_PER_STEP),
      *([cache_v] * PAGES_PER_STEP), *([lfp] * PAGES_PER_STEP))


def _dec_ssm_kernel(xsh_ref, xsht_ref, cw_ref, cwt_ref, cb_ref, cbt_ref, dt_ref, alog_ref, zt_ref,
                    dskt_ref, gt_ref, st_ref, sto_ref, yt_ref, y_sc):
    xc = cb_ref[...]
    xct = cbt_ref[...]
    for j in range(CONV_W):
        xc = xc + xsh_ref[0, j] * cw_ref[j:j + 1, :]
        xct = xct + xsht_ref[0, j] * cwt_ref[:, j:j + 1]
    xc = _silu(xc)
    xct = _silu(xct)
    dt = dt_ref[0]
    da = jnp.exp(dt * (-jnp.exp(alog_ref[...])))
    for hh in range(H_SSM):
        g = hh // R_SSM
        s = st_ref[0, 0, hh]
        for t in range(DEC_SEQ):
            b = xc[t:t + 1, D_SSM + g * N_STATE:D_SSM + (g + 1) * N_STATE]
            c = xc[t:t + 1, D_SSM + (G_SSM + g) * N_STATE:D_SSM + (G_SSM + g + 1) * N_STATE]
            xcol = xct[hh * HD_SSM:(hh + 1) * HD_SSM, t:t + 1]
            s = s * da[t:t + 1, hh:hh + 1] + (xcol * dt[t:t + 1, hh:hh + 1]) * b
            y_sc[hh * HD_SSM:(hh + 1) * HD_SSM, t:t + 1] = jnp.sum(s * c, axis=1, keepdims=True)
        sto_ref[0, hh] = s
    y = (y_sc[...] + dskt_ref[...] * xct[0:D_SSM, :]) * _silu(zt_ref[0])
    half = D_SSM // G_SSM
    parts = []
    for g in range(G_SSM):
        yg = y[g * half:(g + 1) * half, :]
        parts.append(yg * lax.rsqrt(jnp.mean(yg * yg, axis=0, keepdims=True) + EPS))
    yt_ref[0] = jnp.concatenate(parts, axis=0) * gt_ref[...]


def _dec_ssm(layer, xsh, xsht, cw, cwt, cb, cbt, dt_s, alog, zt, dskt, gt, state_ssm):
    per_seq = lambda a: pl.BlockSpec((1,) + a.shape[1:], lambda b: (b,) + (0,) * (a.ndim - 1))
    full = lambda a: pl.BlockSpec(a.shape, lambda b: (0,) * a.ndim)
    return pl.pallas_call(
        _dec_ssm_kernel,
        grid=(DEC_BATCH,),
        in_specs=[per_seq(xsh), per_seq(xsht), full(cw), full(cwt), full(cb), full(cbt), per_seq(dt_s),
                  full(alog), per_seq(zt), full(dskt), full(gt),
                  pl.BlockSpec((1, 1, H_SSM, HD_SSM, N_STATE), lambda b: (layer, b, 0, 0, 0))],
        out_specs=(pl.BlockSpec((1, H_SSM, HD_SSM, N_STATE), lambda b: (b, 0, 0, 0)),
                   pl.BlockSpec((1, D_SSM, DEC_SEQ), lambda b: (b, 0, 0))),
        out_shape=(jax.ShapeDtypeStruct((DEC_BATCH, H_SSM, HD_SSM, N_STATE), F32),
                   jax.ShapeDtypeStruct((DEC_BATCH, D_SSM, DEC_SEQ), F32)),
        scratch_shapes=[pltpu.VMEM((D_SSM, DEC_SEQ), F32)],
        compiler_params=pltpu.CompilerParams(dimension_semantics=("arbitrary",),
                                             vmem_limit_bytes=VMEM_LIMIT),
        name="dec_ssm",
    )(xsh, xsht, cw, cwt, cb, cbt, dt_s, alog, zt, dskt, gt, state_ssm)


def _small_row(*pieces):
    v = jnp.concatenate([p.astype(F32) for p in pieces])
    return jnp.pad(v, (0, SMALL_W - v.shape[0]))[None, :]


def kernel(x_prompt, x_sample, cache_k, cache_v, cache_logf, state_ssm, state_conv, page_table, meta,
           w_in, b_f, dt_bias, a_log, d_skip, conv_w, conv_b, g_pre_mix, g_post_mix, g_att_out, g_ssm_out,
           w_out, g_pre_ffn, g_post_ffn, w_gate, w_up, w_down):
    assert x_prompt.shape == (BATCH, SEQ, D_MODEL) and x_sample.shape == (DEC_BATCH, DEC_SEQ, D_MODEL)
    assert page_table.shape == (DEC_BATCH, N_PAGES)
    n_pool = cache_k.shape[1]

    xp = jnp.concatenate([jnp.broadcast_to(meta.astype(F32)[None], (BATCH, N_META, D_MODEL)), x_prompt,
                          jnp.zeros((BATCH, LP - L_VALID, D_MODEL), F32)], axis=1)
    x = xp.reshape(M_ROWS, D_MODEL)
    x = lax.dynamic_update_slice(x, x_sample.reshape(N_SAMPLE, D_MODEL), (SAMPLE_ROW0, 0))

    page_flat = page_table.reshape(-1).astype(jnp.int32)
    lfp = cache_logf.reshape(DEPTH, n_pool, LF_ROWS, LANES)
    hmask = jnp.where((jnp.arange(NROW_DEC)[:, None] % H_ATT) == (jnp.arange(PAGE_ROWS)[None, :] % H_ATT),
                      0.0, NEG_INF).astype(F32)

    f0, f1 = 3 * D_ATT, 3 * D_ATT + H_ATT
    z0, z1 = f1, f1 + D_SSM
    c0, c1 = z1, z1 + C_CONV
    srows = slice(SAMPLE_ROW0, SAMPLE_ROW0 + N_SAMPLE)

    w_in_bf = jnp.concatenate(
        [w_in[:, :, :f0], w_in[:, :, z0:z1], w_in[:, :, c0:c1], w_in[:, :, f0:f1], w_in[:, :, c1:],
         jnp.zeros((DEPTH, D_MODEL, SMALL_W - H_ATT - H_SSM), w_in.dtype)], axis=2).astype(BF16)
    w_out_bf, w_gate_bf, w_up_bf, w_down_bf = (w.astype(BF16) for w in (w_out, w_gate, w_up, w_down))

    e_head, e_chunk = _head_expansion(HD_SSM), _head_expansion(Q_SSD)

    outs = {k: [] for k in ("k_p", "v_p", "lf_p", "k_s", "v_s", "lf_s", "ssm_p", "ssm_s", "conv_p", "conv_s")}
    for i in range(DEPTH):
        bias = _small_row(b_f[i], dt_bias[i])
        q_bf, k_bf, vt_bf, k_p, v_p, k_s, v_s, z, xbc, small, fcs = _inproj(
            i, x, g_pre_mix[i][None, :], w_in_bf, bias)

        att = _attn_prompt(q_bf, k_bf, vt_bf, fcs)
        alog_row = jnp.pad(a_log[i].astype(F32), (DT_LANE0, SMALL_W - DT_LANE0 - H_SSM))[None, :]
        dsk_row = jnp.repeat(d_skip[i].astype(F32), HD_SSM)[None, :]
        g_ssm_row = g_ssm_out[i].astype(F32)[None, :]
        y_ssm, st_p = _ssd_prompt(xbc, small, z, conv_w[i], conv_b[i][None, :], alog_row, dsk_row, g_ssm_row,
                                  e_head, e_chunk)

        q_s = q_bf[srows].reshape(DEC_BATCH, NROW_DEC, HD_ATT)
        kn = k_s.reshape(DEC_BATCH, NROW_DEC, HD_ATT)
        vn = v_s.reshape(DEC_BATCH, NROW_DEC, HD_ATT)
        lf_s = small[srows, :H_ATT].reshape(DEC_BATCH, DEC_SEQ, H_ATT)
        att_s = _dec_attn(i, page_flat, q_s, kn, vn, lf_s.reshape(DEC_BATCH, NROW_DEC, 1),
                          lf_s.reshape(DEC_BATCH, 1, NROW_DEC), hmask, cache_k, cache_v, lfp[i])
        xbc_s = xbc[srows].reshape(DEC_BATCH, DEC_SEQ, C_CONV)
        xpad = jnp.concatenate([state_conv[i].astype(F32), xbc_s], axis=1)
        xsh = jnp.stack([xpad[:, j:j + DEC_SEQ] for j in range(CONV_W)], axis=1)
        xsht = xsh.transpose(0, 1, 3, 2)
        dt_s = small[srows, DT_LANE0:DT_LANE0 + H_SSM].reshape(DEC_BATCH, DEC_SEQ, H_SSM)
        zt = z[srows].reshape(DEC_BATCH, DEC_SEQ, D_SSM).transpose(0, 2, 1)
        st_s, y_s_t = _dec_ssm(i, xsh, xsht, conv_w[i], conv_w[i].T, conv_b[i][None, :], conv_b[i][:, None],
                               dt_s, a_log[i].astype(F32)[None, :], zt, dsk_row.T, g_ssm_row.T, state_ssm)

        x1, h2 = _outproj(i, att, y_ssm, att_s.reshape(N_SAMPLE, D_ATT),
                          y_s_t.transpose(0, 2, 1).reshape(N_SAMPLE, D_SSM), x, w_out_bf,
                          g_att_out[i][None, :], g_post_mix[i][None, :], g_pre_ffn[i][None, :])
        x = _ffn(i, h2, w_gate_bf, w_up_bf, w_down_bf, x1, g_post_ffn[i][None, :])

        outs["k_p"].append(k_p.reshape(BATCH, L_VALID, H_ATT, HD_ATT))
        outs["v_p"].append(v_p.reshape(BATCH, L_VALID, H_ATT, HD_ATT))
        outs["lf_p"].append(small[:, :H_ATT].reshape(BATCH, LP, H_ATT)[:, :L_VALID])
        outs["k_s"].append(kn.reshape(DEC_BATCH, DEC_SEQ, H_ATT, HD_ATT))
        outs["v_s"].append(vn.reshape(DEC_BATCH, DEC_SEQ, H_ATT, HD_ATT))
        outs["lf_s"].append(lf_s)
        outs["ssm_p"].append(st_p)
        outs["ssm_s"].append(st_s)
        outs["conv_p"].append(xbc.reshape(BATCH, LP, C_CONV)[:, L_VALID - (CONV_W - 1):L_VALID])
        outs["conv_s"].append(xpad[:, DEC_SEQ:])

    xf = x.reshape(BATCH, LP, D_MODEL)
    y_prompt = xf[:, N_META:L_VALID]
    y_sample = x[srows].reshape(DEC_BATCH, DEC_SEQ, D_MODEL)
    st = lambda k: jnp.stack(outs[k])
    return (y_prompt, y_sample, st("k_p"), st("v_p"), st("lf_p"), st("k_s"), st("v_s"), st("lf_s"),
            st("ssm_p"), st("ssm_s"), st("conv_p"), st("conv_s"))
```

```python
import functools
import math

import jax
import jax.numpy as jnp
from jax import lax
from jax.experimental import pallas as pl
from jax.experimental.pallas import tpu as pltpu

D_MODEL = 2048
BATCH = 2
SEQ = 4096
DEPTH = 4
DEC_BATCH = 8
DEC_SEQ = 4
PAST_LEN = 16384
PAGE_SIZE = 128
N_META = 16
H_ATT = 8
HD_ATT = 128
D_ATT = H_ATT * HD_ATT
H_SSM = 16
HD_SSM = 64
D_SSM = H_SSM * HD_SSM
G_SSM = 2
R_SSM = H_SSM // G_SSM
N_STATE = 128
CONV_W = 4
C_CONV = D_SSM + 2 * G_SSM * N_STATE
D_MIX = D_ATT + D_SSM
D_FF = ((8 * D_MODEL + 3 * 256 - 1) // (3 * 256)) * 256
EPS = 1e-6
ATT_SCALE = HD_ATT ** -0.5

LANES = 128
SUBLANES = 8
L_VALID = N_META + SEQ
LP = 4352
M_ROWS = BATCH * LP
N_SAMPLE = DEC_BATCH * DEC_SEQ
SAMPLE_LOCAL = 4128
SAMPLE_ROW0 = (BATCH - 1) * LP + SAMPLE_LOCAL
N_PAGES = PAST_LEN // PAGE_SIZE
SMALL_W = LANES
D_IN = 3 * D_ATT + H_ATT + D_SSM + C_CONV + H_SSM
DT_LANE0 = H_ATT

TM_PROJ = 256
TM_FFN = 512
TF_FFN = 512
TQ = 512
TK = 512
TK_DIAG = 256
HEADS_PER_STEP = 2
NQ_FULL = L_VALID // TQ - 1
TQ_LAST = -(-(L_VALID - NQ_FULL * TQ) // LANES) * LANES
ATT_ROWS = NQ_FULL * TQ + TQ_LAST
LOG2E = math.log2(math.e)
Q_SSD = 128
VMEM_LIMIT = 56 * 1024 * 1024

F32 = jnp.float32
BF16 = jnp.bfloat16
NEG_INF = float("-inf")


def _nt_dot(a, b):
    return lax.dot_general(a, b, (((1,), (1,)), ((), ())), preferred_element_type=F32)


def _silu(x):
    return x * (1.0 / (1.0 + jnp.exp(-x)))


def _rms(x, g):
    return x * lax.rsqrt(jnp.mean(x * x, axis=-1, keepdims=True) + EPS) * g


def _cumsum_rows(x):
    n = x.shape[0]
    row = lax.broadcasted_iota(jnp.int32, x.shape, 0)
    s = 1
    while s < n:
        x = x + jnp.where(row >= s, pltpu.roll(x, s, 0), 0.0)
        s *= 2
    return x


def _inproj_kernel(x_ref, g_ref, w_ref, b_ref, q_ref, kb_ref, vt_ref, kp_ref, vp_ref, ks_ref, vs_ref,
                   z_ref, xbc_ref, sm_ref, fcs_ref, carry_sc):
    i = pl.program_id(0)
    h = _rms(x_ref[...], g_ref[...]).astype(BF16)

    def mm(c0, c1):
        return jnp.dot(h, w_ref[0, :, c0:c1], preferred_element_type=F32)

    q_ref[...] = ((mm(0, D_ATT) * ATT_SCALE) * LOG2E).astype(BF16)
    k = mm(D_ATT, 2 * D_ATT)
    kp_ref[0] = k
    kb_ref[...] = k.astype(BF16)
    v = mm(2 * D_ATT, 3 * D_ATT)
    vp_ref[0] = v
    vt_ref[...] = v.T.astype(BF16)

    @pl.when(i == SAMPLE_ROW0 // TM_PROJ)
    def _():
        r0 = SAMPLE_ROW0 % TM_PROJ
        ks_ref[...] = k[r0:r0 + N_SAMPLE]
        vs_ref[...] = v[r0:r0 + N_SAMPLE]

    rest = mm(3 * D_ATT, D_IN)
    z_ref[...] = rest[:, H_ATT:H_ATT + D_SSM]
    xbc_ref[...] = rest[:, H_ATT + D_SSM:H_ATT + D_SSM + C_CONV]
    lane = lax.broadcasted_iota(jnp.int32, (TM_PROJ, SMALL_W), 1)
    tail0 = (H_ATT + D_SSM + C_CONV) // LANES * LANES
    last = jnp.concatenate([rest[:, tail0:], jnp.zeros((TM_PROJ, SMALL_W - (D_IN - 3 * D_ATT - tail0)), F32)],
                           axis=1)
    u = jnp.where(lane < H_ATT, rest[:, 0:SMALL_W], jnp.where(lane < H_ATT + H_SSM, last, 0.0)) + b_ref[...]
    t = jnp.log(1.0 + jnp.exp(-jnp.abs(u)))
    sm = jnp.where(lane < H_ATT, jnp.minimum(u, 0.0) - t, jnp.maximum(u, 0.0) + t)
    sm_ref[...] = sm

    @pl.when(i % (LP // TM_PROJ) == 0)
    def _():
        carry_sc[...] = jnp.zeros_like(carry_sc)

    cs = _cumsum_rows(sm) + carry_sc[...]
    fcs_ref[...] = cs
    carry_sc[...] = cs[TM_PROJ - 1:TM_PROJ, :]


def _inproj(layer, x, g, w, bias):
    n = M_ROWS // TM_PROJ
    row = lambda w_: pl.BlockSpec((TM_PROJ, w_), lambda i: (i, 0))
    full = lambda a: pl.BlockSpec(a.shape, lambda i: (0,) * a.ndim)
    tiles_per_seq = LP // TM_PROJ
    cache = pl.BlockSpec((1, TM_PROJ, D_ATT), lambda i: (i // tiles_per_seq, i % tiles_per_seq, 0))
    sample = pl.BlockSpec((N_SAMPLE, D_ATT), lambda i: (0, 0))
    outs = (
        (jax.ShapeDtypeStruct((M_ROWS, D_ATT), BF16), row(D_ATT)),
        (jax.ShapeDtypeStruct((M_ROWS, D_ATT), BF16), row(D_ATT)),
        (jax.ShapeDtypeStruct((D_ATT, M_ROWS), BF16),
         pl.BlockSpec((D_ATT, TM_PROJ), lambda i: (0, i))),
        (jax.ShapeDtypeStruct((BATCH, L_VALID, D_ATT), F32), cache),
        (jax.ShapeDtypeStruct((BATCH, L_VALID, D_ATT), F32), cache),
        (jax.ShapeDtypeStruct((N_SAMPLE, D_ATT), F32), sample),
        (jax.ShapeDtypeStruct((N_SAMPLE, D_ATT), F32), sample),
        (jax.ShapeDtypeStruct((M_ROWS, D_SSM), F32), row(D_SSM)),
        (jax.ShapeDtypeStruct((M_ROWS, C_CONV), F32), row(C_CONV)),
        (jax.ShapeDtypeStruct((M_ROWS, SMALL_W), F32), row(SMALL_W)),
        (jax.ShapeDtypeStruct((M_ROWS, SMALL_W), F32), row(SMALL_W)),
    )
    return pl.pallas_call(
        _inproj_kernel,
        grid=(n,),
        in_specs=[row(D_MODEL), full(g),
                  pl.BlockSpec((1,) + w.shape[1:], lambda i: (layer, 0, 0), pipeline_mode=pl.Buffered(1)),
                  full(bias)],
        out_specs=tuple(o[1] for o in outs),
        out_shape=tuple(o[0] for o in outs),
        scratch_shapes=[pltpu.VMEM((1, SMALL_W), F32)],
        compiler_params=pltpu.CompilerParams(dimension_semantics=("arbitrary",),
                                             vmem_limit_bytes=VMEM_LIMIT),
        name="inproj",
    )(x, g, w, bias)


def _attn_step(hh, ks, tk, q0, lo, w, masked, q_ref, k_ref, fkb_sc, vt_ref, m_sc, l_sc, acc_sc):
    s = _attn_scores(hh, ks, tk, q0 + lo, w, q_ref, k_ref)
    _attn_accumulate(hh, s, ks, tk, lo, w, masked, fkb_sc, vt_ref, m_sc, l_sc, acc_sc)


def _attn_scores(hh, ks, tk, qs, w, q_ref, k_ref):
    hs = slice(hh * HD_ATT, (hh + 1) * HD_ATT)
    return _nt_dot(k_ref[pl.ds(ks, tk), hs], q_ref[pl.ds(qs, w), hs])


def _attn_accumulate(hh, s, ks, tk, lo, w, masked, fkb_sc, vt_ref, m_sc, l_sc, acc_sc):
    s = s - jnp.concatenate([fkb_sc[hh, pl.ds(ks, tk), :]] * (w // LANES), axis=1)
    if masked:
        s = jnp.where(lax.broadcasted_iota(jnp.int32, (tk, w), 0) <= lax.broadcasted_iota(jnp.int32, (tk, w), 1),
                      s, NEG_INF)
    m_prev = m_sc[hh, :, lo:lo + w]
    m_new = jnp.maximum(m_prev, jnp.max(s, axis=0, keepdims=True))
    alpha = jnp.exp2(m_prev - m_new)
    p = jnp.exp2(s - m_new)
    l_sc[hh, :, lo:lo + w] = alpha * l_sc[hh, :, lo:lo + w] + jnp.sum(p, axis=0, keepdims=True)
    acc_sc[hh, :, lo:lo + w] = alpha * acc_sc[hh, :, lo:lo + w] + jnp.dot(
        vt_ref[hh * HD_ATT:(hh + 1) * HD_ATT, pl.ds(ks, tk)], p.astype(BF16), preferred_element_type=F32)
    m_sc[hh, :, lo:lo + w] = m_new


def _attn_kernel(q_ref, k_ref, vt_ref, fcs_ref, o_ref, fkb_sc, s_sc, m_sc, l_sc, acc_sc):
    hp = pl.program_id(1)
    lane = lax.broadcasted_iota(jnp.int32, (TK_DIAG, LANES), 1)
    heads = range(HEADS_PER_STEP)

    def prep(c, carry):
        r0 = pl.multiple_of(c * TK_DIAG, TK_DIAG)
        f = fcs_ref[pl.ds(r0, TK_DIAG), :]
        for hh in heads:
            col = jnp.sum(jnp.where(lane == hp * HEADS_PER_STEP + hh, f, 0.0), axis=1, keepdims=True) * LOG2E
            fkb_sc[hh, pl.ds(r0, TK_DIAG), :] = jnp.broadcast_to(col, (TK_DIAG, LANES))
        return carry

    lax.fori_loop(0, LP // TK_DIAG, prep, 0)
    acc_refs = (fkb_sc, vt_ref, m_sc, l_sc, acc_sc)

    def q_tile(q0, w, n_unmasked):
        n = jnp.asarray(n_unmasked, jnp.int32)
        for hh in heads:
            m_sc[hh, :, 0:w] = jnp.full((1, w), NEG_INF, F32)
            l_sc[hh, :, 0:w] = jnp.zeros((1, w), F32)
            acc_sc[hh, :, 0:w] = jnp.zeros((HD_ATT, w), F32)

        def scores_to(slot, j):
            for hh in heads:
                s_sc[slot, hh, :, 0:w] = _attn_scores(hh, pl.multiple_of(j * TK, TK), TK, q0, w, q_ref, k_ref)

        def accumulate_from(slot, j):
            for hh in heads:
                _attn_accumulate(hh, s_sc[slot, hh, :, 0:w], pl.multiple_of(j * TK, TK), TK, 0, w, False,
                                 *acc_refs)

        @pl.when(n > 0)
        def _():
            scores_to(0, 0)

        def body(jj, carry):
            scores_to(1, 2 * jj + 1)
            accumulate_from(0, 2 * jj)
            scores_to(0, 2 * jj + 2)
            accumulate_from(1, 2 * jj + 1)
            return carry

        n_pairs = jnp.maximum(n - 1, 0) // 2
        lax.fori_loop(0, n_pairs, body, 0)

        @pl.when((n > 0) & ((n & 1) == 1))
        def _():
            accumulate_from(0, 2 * n_pairs)

        @pl.when((n > 0) & ((n & 1) == 0))
        def _():
            scores_to(1, 2 * n_pairs + 1)
            accumulate_from(0, 2 * n_pairs)
            accumulate_from(1, 2 * n_pairs + 1)

        for lo in range(0, w, TK_DIAG):
            for hh in heads:
                _attn_step(hh, q0 + lo, TK_DIAG, q0, lo, w - lo, True, q_ref, k_ref, *acc_refs)
        for hh in heads:
            o_ref[pl.ds(q0, w), hh * HD_ATT:(hh + 1) * HD_ATT] = (
                acc_sc[hh, :, 0:w] * (1.0 / l_sc[hh, :, 0:w])).T

    def full_tile(i, carry):
        q_tile(pl.multiple_of(i * TQ, TQ), TQ, i * (TQ // TK))
        return carry

    lax.fori_loop(0, NQ_FULL, full_tile, 0)
    q_tile(NQ_FULL * TQ, TQ_LAST, NQ_FULL * TQ // TK)
    o_ref[ATT_ROWS:LP, :] = jnp.zeros((LP - ATT_ROWS, HEADS_PER_STEP * HD_ATT), F32)


def _attn_prompt(q_bf, k_bf, vt_bf, fcs):
    blk = pl.BlockSpec((LP, HEADS_PER_STEP * HD_ATT), lambda b, h: (b, h))
    return pl.pallas_call(
        _attn_kernel,
        grid=(BATCH, H_ATT // HEADS_PER_STEP),
        in_specs=[blk, blk, pl.BlockSpec((HEADS_PER_STEP * HD_ATT, LP), lambda b, h: (h, b)),
                  pl.BlockSpec((LP, SMALL_W), lambda b, h: (b, 0))],
        out_specs=blk,
        out_shape=jax.ShapeDtypeStruct((M_ROWS, D_ATT), F32),
        scratch_shapes=[pltpu.VMEM((HEADS_PER_STEP, LP, LANES), F32),
                        pltpu.VMEM((2, HEADS_PER_STEP, TK, TQ_LAST), F32),
                        pltpu.VMEM((HEADS_PER_STEP, 1, TQ_LAST), F32),
                        pltpu.VMEM((HEADS_PER_STEP, 1, TQ_LAST), F32),
                        pltpu.VMEM((HEADS_PER_STEP, HD_ATT, TQ_LAST), F32)],
        compiler_params=pltpu.CompilerParams(dimension_semantics=("arbitrary", "arbitrary"),
                                             vmem_limit_bytes=VMEM_LIMIT),
        name="attn_prompt",
    )(q_bf, k_bf, vt_bf, fcs)


PAIR_W = 2 * HD_SSM


def _expand(v, e_ref):
    hi = v.astype(BF16)
    r1 = v - hi.astype(F32)
    mid = r1.astype(BF16)
    lo = (r1 - mid.astype(F32)).astype(BF16)
    e = e_ref[...]
    return (jnp.dot(hi, e, preferred_element_type=F32) + jnp.dot(mid, e, preferred_element_type=F32)) + jnp.dot(
        lo, e, preferred_element_type=F32)


def _head_expansion(width):
    r = jnp.arange(SMALL_W)[:, None]
    c = jnp.arange(H_SSM * width)[None, :]
    return (r == DT_LANE0 + c // width).astype(BF16)


def _ssd_kernel(xbc_ref, sm_ref, z_ref, cw_ref, cb_ref, alog_ref, dsk_ref, g_ref, e_ref, e2_ref, y_ref, st_ref,
                tail_sc, state_sc):
    c = pl.program_id(1)
    nc = pl.num_programs(1)
    Q = Q_SSD

    @pl.when(c == 0)
    def _():
        tail_sc[...] = jnp.zeros_like(tail_sc)
        state_sc[...] = jnp.zeros_like(state_sc)

    x = xbc_ref[...]
    pt = tail_sc[...]
    row8 = lax.broadcasted_iota(jnp.int32, (SUBLANES, C_CONV), 0)
    acc = x * cw_ref[CONV_W - 1:CONV_W, :]
    for k in range(1, CONV_W):
        r = pltpu.roll(x, k, 0)
        first = jnp.where(row8 < k, pltpu.roll(pt, k, 0), r[0:SUBLANES])
        shifted = jnp.concatenate([first, r[SUBLANES:]], axis=0)
        acc = acc + shifted * cw_ref[CONV_W - 1 - k:CONV_W - k, :]
    tail_sc[...] = x[Q - SUBLANES:Q]
    xc = _silu(acc + cb_ref[...])
    xs = xc[:, :D_SSM]

    pos = c * Q + lax.broadcasted_iota(jnp.int32, (Q, LANES), 0)
    lane = lax.broadcasted_iota(jnp.int32, (Q, LANES), 1)
    is_dt = (pos < L_VALID) & (lane >= DT_LANE0) & (lane < DT_LANE0 + H_SSM)
    dt = jnp.where(is_dt, sm_ref[...], 0.0)
    a_dt = dt * (-jnp.exp(alog_ref[...]))
    acs = _cumsum_rows(a_dt)
    acs_t = acs.T
    last = acs[Q - 1:Q, :]
    chunk_decay = jnp.exp(last)
    x_dt = xs * _expand(dt, e_ref)
    xd_t = (x_dt * _expand(jnp.exp(last - acs), e_ref)).T
    dout_x = _expand(jnp.exp(acs), e_ref)
    acs_x = _expand(acs, e2_ref)
    tril = (lax.broadcasted_iota(jnp.int32, (Q, Q), 0) >= lax.broadcasted_iota(jnp.int32, (Q, Q), 1))
    low = lane < HD_SSM
    rowi = lax.broadcasted_iota(jnp.int32, (PAIR_W, N_STATE), 0)

    ys = []
    for g in range(G_SSM):
        bg = xc[:, D_SSM + g * N_STATE:D_SSM + (g + 1) * N_STATE].astype(BF16)
        cg = xc[:, D_SSM + (G_SSM + g) * N_STATE:D_SSM + (G_SSM + g + 1) * N_STATE].astype(BF16)
        cb = _nt_dot(cg, bg)
        for pr in range(R_SSM // 2):
            h0 = g * R_SSM + 2 * pr
            pi = h0 // 2
            ps = slice(pi * PAIR_W, (pi + 1) * PAIR_W)
            ms = []
            for hh in (h0, h0 + 1):
                col = DT_LANE0 + hh
                seg = jnp.exp(jnp.where(tril, acs_x[:, hh * Q:(hh + 1) * Q] - acs_t[col:col + 1, :], NEG_INF))
                ms.append((cb * seg).astype(BF16))
            xp = x_dt[:, ps]
            rhs = jnp.concatenate([jnp.where(low, xp, 0.0), jnp.where(low, 0.0, xp)], axis=0).astype(BF16)
            y_diag = jnp.dot(jnp.concatenate(ms, axis=1), rhs, preferred_element_type=F32)
            st = state_sc[pi]
            y_off = _nt_dot(cg, st.astype(BF16)) * dout_x[:, ps]
            ys.append(y_diag + y_off)
            new = jnp.dot(xd_t[ps, :].astype(BF16), bg, preferred_element_type=F32)
            c0 = DT_LANE0 + h0
            cd = jnp.where(rowi < HD_SSM, chunk_decay[:, c0:c0 + 1], chunk_decay[:, c0 + 1:c0 + 2])
            state_sc[pi] = st * cd + new

    y = (jnp.concatenate(ys, axis=1) + dsk_ref[...] * xs) * _silu(z_ref[...])
    half = D_SSM // G_SSM
    parts = []
    for g in range(G_SSM):
        yg = y[:, g * half:(g + 1) * half]
        parts.append(yg * lax.rsqrt(jnp.mean(yg * yg, axis=-1, keepdims=True) + EPS))
    y_ref[...] = jnp.concatenate(parts, axis=1) * g_ref[...]

    @pl.when(c == nc - 1)
    def _():
        st_ref[0] = state_sc[...].reshape(H_SSM, HD_SSM, N_STATE)


def _ssd_prompt(xbc, small, z, conv_w, conv_b, alog_row, dsk_row, g_row, e1, e2):
    nc = LP // Q_SSD
    row = lambda w_: pl.BlockSpec((Q_SSD, w_), lambda b, c: (b * nc + c, 0))
    full = lambda a: pl.BlockSpec(a.shape, lambda b, c: (0,) * a.ndim)
    return pl.pallas_call(
        _ssd_kernel,
        grid=(BATCH, nc),
        in_specs=[row(C_CONV), row(SMALL_W), row(D_SSM), full(conv_w), full(conv_b), full(alog_row),
                  full(dsk_row), full(g_row), full(e1), full(e2)],
        out_specs=(row(D_SSM),
                   pl.BlockSpec((1, H_SSM, HD_SSM, N_STATE), lambda b, c: (b, 0, 0, 0))),
        out_shape=(jax.ShapeDtypeStruct((M_ROWS, D_SSM), F32),
                   jax.ShapeDtypeStruct((BATCH, H_SSM, HD_SSM, N_STATE), F32)),
        scratch_shapes=[pltpu.VMEM((SUBLANES, C_CONV), F32),
                        pltpu.VMEM((H_SSM // 2, PAIR_W, N_STATE), F32)],
        compiler_params=pltpu.CompilerParams(dimension_semantics=("arbitrary", "arbitrary"),
                                             vmem_limit_bytes=VMEM_LIMIT),
        name="ssd_prompt",
    )(xbc, small, z, conv_w, conv_b, alog_row, dsk_row, g_row, e1, e2)


def _outproj_kernel(att_ref, y_ref, atts_ref, ys_ref, x_ref, w_ref, ga_ref, gm_ref, gf_ref, x1_ref, h2_ref):
    r0 = SAMPLE_ROW0 % TM_PROJ
    rows = pl.program_id(0) * TM_PROJ + lax.broadcasted_iota(jnp.int32, (TM_PROJ, 1), 0)
    is_sample = (rows >= SAMPLE_ROW0) & (rows < SAMPLE_ROW0 + N_SAMPLE)

    def merged(main, small):
        placed = jnp.concatenate([jnp.zeros((r0, small.shape[1]), F32), small,
                                  jnp.zeros((TM_PROJ - r0 - N_SAMPLE, small.shape[1]), F32)], axis=0)
        return jnp.where(is_sample, placed, main)

    a = _rms(merged(att_ref[...], atts_ref[...]), ga_ref[...]).astype(BF16)
    y = merged(y_ref[...], ys_ref[...])
    mixed = jnp.dot(a, w_ref[0, 0:D_ATT, :], preferred_element_type=F32)
    mixed = mixed + jnp.dot(y.astype(BF16), w_ref[0, D_ATT:D_MIX, :], preferred_element_type=F32)
    x1 = x_ref[...] + _rms(mixed, gm_ref[...])
    x1_ref[...] = x1
    h2_ref[...] = _rms(x1, gf_ref[...]).astype(BF16)


def _outproj(layer, att, y, att_s, y_s, x, w, g_att, g_post_mix, g_pre_ffn):
    n = M_ROWS // TM_PROJ
    row = lambda w_: pl.BlockSpec((TM_PROJ, w_), lambda i: (i, 0))
    full = lambda a: pl.BlockSpec(a.shape, lambda i: (0,) * a.ndim)
    return pl.pallas_call(
        _outproj_kernel,
        grid=(n,),
        in_specs=[row(D_ATT), row(D_SSM), full(att_s), full(y_s), row(D_MODEL),
                  pl.BlockSpec((1,) + w.shape[1:], lambda i: (layer, 0, 0), pipeline_mode=pl.Buffered(1)),
                  full(g_att), full(g_post_mix), full(g_pre_ffn)],
        out_specs=(row(D_MODEL), row(D_MODEL)),
        out_shape=(jax.ShapeDtypeStruct((M_ROWS, D_MODEL), F32),
                   jax.ShapeDtypeStruct((M_ROWS, D_MODEL), BF16)),
        compiler_params=pltpu.CompilerParams(dimension_semantics=("arbitrary",),
                                             vmem_limit_bytes=VMEM_LIMIT),
        name="outproj",
    )(att, y, att_s, y_s, x, w, g_att, g_post_mix, g_pre_ffn)


def _ffn_kernel(h_ref, wg_ref, wu_ref, wd_ref, x1_ref, g_ref, o_ref):
    j = pl.program_id(1)

    @pl.when(j == 0)
    def _():
        o_ref[...] = jnp.zeros_like(o_ref)

    h = h_ref[...]
    gate = jnp.dot(h, wg_ref[0], preferred_element_type=F32)
    up = jnp.dot(h, wu_ref[0], preferred_element_type=F32)
    act = (_silu(gate) * up).astype(BF16)
    o_ref[...] += jnp.dot(act, wd_ref[0], preferred_element_type=F32)

    @pl.when(j == pl.num_programs(1) - 1)
    def _():
        o_ref[...] = x1_ref[...] + _rms(o_ref[...], g_ref[...])


def _ffn(layer, h2, wg, wu, wd, x1, g_post_ffn):
    return pl.pallas_call(
        _ffn_kernel,
        grid=(M_ROWS // TM_FFN, D_FF // TF_FFN),
        in_specs=[pl.BlockSpec((TM_FFN, D_MODEL), lambda i, j: (i, 0)),
                  pl.BlockSpec((1, D_MODEL, TF_FFN), lambda i, j: (layer, 0, j)),
                  pl.BlockSpec((1, D_MODEL, TF_FFN), lambda i, j: (layer, 0, j)),
                  pl.BlockSpec((1, TF_FFN, D_MODEL), lambda i, j: (layer, j, 0)),
                  pl.BlockSpec((TM_FFN, D_MODEL), lambda i, j: (i, 0)),
                  pl.BlockSpec(g_post_ffn.shape, lambda i, j: (0, 0))],
        out_specs=pl.BlockSpec((TM_FFN, D_MODEL), lambda i, j: (i, 0)),
        out_shape=jax.ShapeDtypeStruct((M_ROWS, D_MODEL), F32),
        compiler_params=pltpu.CompilerParams(dimension_semantics=("arbitrary", "arbitrary"),
                                             vmem_limit_bytes=VMEM_LIMIT),
        name="ffn",
    )(h2, wg, wu, wd, x1, g_post_ffn)


NROW_DEC = DEC_SEQ * H_ATT
PAGE_ROWS = PAGE_SIZE * H_ATT
LF_ROWS = PAGE_ROWS // LANES
PAGES_PER_STEP = 8


def _dec_attn_kernel(pt_ref, q_ref, kn_ref, vn_ref, lfc_ref, lfr_ref, hmask_ref, *rest):
    del pt_ref
    n = PAGES_PER_STEP
    kp_refs, vp_refs, lfp_refs = rest[0:n], rest[n:2 * n], rest[2 * n:3 * n]
    o_ref, m_sc, l_sc, acc_sc, carry_sc, cn_sc = rest[3 * n:]
    j = pl.program_id(1)
    q = q_ref[0]

    def update(s, v_bf):
        m_prev = m_sc[...]
        m_new = jnp.maximum(m_prev, jnp.max(s, axis=1, keepdims=True))
        alpha = jnp.exp2(m_prev - m_new)
        p = jnp.exp2(s - m_new)
        l_sc[...] = alpha * l_sc[...] + jnp.sum(p, axis=1, keepdims=True)
        acc_sc[...] = alpha * acc_sc[...] + jnp.dot(p.astype(BF16), v_bf, preferred_element_type=F32)
        m_sc[...] = m_new

    @pl.when(j == 0)
    def _():
        r = lax.broadcasted_iota(jnp.int32, (NROW_DEC, NROW_DEC), 0)
        c = lax.broadcasted_iota(jnp.int32, (NROW_DEC, NROW_DEC), 1)
        same = (r & (H_ATT - 1)) == (c & (H_ATT - 1))
        keep = same & ((c >> 3) <= (r >> 3))
        cn_col = jnp.sum(jnp.where(keep, lfr_ref[0] * LOG2E, 0.0), axis=1, keepdims=True)
        cn_row = jnp.sum(jnp.where(same & ((r >> 3) <= (c >> 3)), lfc_ref[0] * LOG2E, 0.0),
                         axis=0, keepdims=True)
        m_sc[...] = jnp.full_like(m_sc, NEG_INF)
        l_sc[...] = jnp.zeros_like(l_sc)
        acc_sc[...] = jnp.zeros_like(acc_sc)
        carry_sc[...] = jnp.zeros_like(carry_sc)
        cn_sc[...] = cn_col
        s = _nt_dot(q, kn_ref[0].astype(BF16)) + cn_col - cn_row
        update(jnp.where(keep, s, NEG_INF), vn_ref[0].astype(BF16))

    lane = lax.broadcasted_iota(jnp.int32, (LF_ROWS, LANES), 1)
    rowi = lax.broadcasted_iota(jnp.int32, (LF_ROWS, LANES), 0)
    carry = carry_sc[...]
    base = cn_sc[...] + hmask_ref[...]
    scores = []
    for i in range(PAGES_PER_STEP):
        lf = lfp_refs[i][0] * LOG2E
        suf = lf
        tot = lf
        s_ = H_ATT
        while s_ < LANES:
            suf = suf + jnp.where(lane < LANES - s_, pltpu.roll(suf, LANES - s_, 1), 0.0)
            tot = tot + pltpu.roll(tot, s_, 1)
            s_ *= 2
        rsuf = tot
        s_ = 1
        while s_ < LF_ROWS:
            rsuf = rsuf + jnp.where(rowi < LF_ROWS - s_, pltpu.roll(rsuf, LF_ROWS - s_, 0), 0.0)
            s_ *= 2
        decay = carry + (suf - lf) + (rsuf - tot)
        carry = carry + rsuf[0:1, :]
        bias = jnp.concatenate([decay[r:r + 1, :] for r in range(LF_ROWS)], axis=1)
        k = kp_refs[i][0, 0].reshape(PAGE_ROWS, HD_ATT).astype(BF16)
        scores.append(_nt_dot(q, k) + (base + bias))
    carry_sc[...] = carry

    m_prev = m_sc[...]
    m_new = m_prev
    for s in scores:
        m_new = jnp.maximum(m_new, jnp.max(s, axis=1, keepdims=True))
    alpha = jnp.exp2(m_prev - m_new)
    l_new = alpha * l_sc[...]
    acc = alpha * acc_sc[...]
    for i, s in enumerate(scores):
        p = jnp.exp2(s - m_new)
        l_new = l_new + jnp.sum(p, axis=1, keepdims=True)
        v = vp_refs[i][0, 0].reshape(PAGE_ROWS, HD_ATT).astype(BF16)
        acc = acc + jnp.dot(p.astype(BF16), v, preferred_element_type=F32)
    l_sc[...] = l_new
    acc_sc[...] = acc
    m_sc[...] = m_new

    @pl.when(j == pl.num_programs(1) - 1)
    def _():
        o_ref[0] = acc_sc[...] * (1.0 / l_sc[...])


def _dec_attn(layer, page_flat, q_s, kn, vn, lf_col, lf_row, hmask, cache_k, cache_v, lfp):
    def page(i):
        return lambda b, j, pt: pt[b * N_PAGES + (N_PAGES - 1 - (j * PAGES_PER_STEP + i))]

    def kv_spec(i):
        pg = page(i)
        return pl.BlockSpec((1, 1, PAGE_SIZE, H_ATT, HD_ATT), lambda b, j, pt: (layer, pg(b, j, pt), 0, 0, 0))

    def lf_spec(i):
        pg = page(i)
        return pl.BlockSpec((1, LF_ROWS, LANES), lambda b, j, pt: (pg(b, j, pt), 0, 0))

    per_seq = lambda a: pl.BlockSpec((1,) + a.shape[1:], lambda b, j, pt: (b,) + (0,) * (a.ndim - 1))
    steps = range(PAGES_PER_STEP)
    grid_spec = pltpu.PrefetchScalarGridSpec(
        num_scalar_prefetch=1,
        grid=(DEC_BATCH, N_PAGES // PAGES_PER_STEP),
        in_specs=[per_seq(q_s), per_seq(kn), per_seq(vn), per_seq(lf_col), per_seq(lf_row),
                  pl.BlockSpec(hmask.shape, lambda b, j, pt: (0, 0))]
                 + [kv_spec(i) for i in steps] + [kv_spec(i) for i in steps] + [lf_spec(i) for i in steps],
        out_specs=pl.BlockSpec((1, NROW_DEC, HD_ATT), lambda b, j, pt: (b, 0, 0)),
        scratch_shapes=[pltpu.VMEM((NROW_DEC, 1), F32), pltpu.VMEM((NROW_DEC, 1), F32),
                        pltpu.VMEM((NROW_DEC, HD_ATT), F32), pltpu.VMEM((1, LANES), F32),
                        pltpu.VMEM((NROW_DEC, 1), F32)],
    )
    return pl.pallas_call(
        _dec_attn_kernel,
        grid_spec=grid_spec,
        out_shape=jax.ShapeDtypeStruct((DEC_BATCH, NROW_DEC, HD_ATT), F32),
        compiler_params=pltpu.CompilerParams(dimension_semantics=("arbitrary", "arbitrary"),
                                             vmem_limit_bytes=VMEM_LIMIT),
        name="dec_attn",
    )(page_flat, q_s, kn, vn, lf_col, lf_row, hmask, *([cache_k] * PAGES_PER_STEP),
      *([cache_v] * PAGES_PER_STEP), *([lfp] * PAGES_PER_STEP))


def _dec_ssm_kernel(xsh_ref, xsht_ref, cw_ref, cwt_ref, cb_ref, cbt_ref, dt_ref, alog_ref, zt_ref,
                    dskt_ref, gt_ref, st_ref, sto_ref, yt_ref, y_sc):
    xc = cb_ref[...]
    xct = cbt_ref[...]
    for j in range(CONV_W):
        xc = xc + xsh_ref[0, j] * cw_ref[j:j + 1, :]
        xct = xct + xsht_ref[0, j] * cwt_ref[:, j:j + 1]
    xc = _silu(xc)
    xct = _silu(xct)
    dtx = dt_ref[0]
    da = jnp.exp(dtx * (-jnp.exp(alog_ref[...])))
    xw = xct[0:D_SSM, :] * dtx
    for pi in range(H_SSM // 2):
        g = (2 * pi) // R_SSM
        rows = slice(pi * PAIR_W, (pi + 1) * PAIR_W)
        s = st_ref[0, 0, 2 * pi:2 * pi + 2].reshape(PAIR_W, N_STATE)
        for t in range(DEC_SEQ):
            b = xc[t:t + 1, D_SSM + g * N_STATE:D_SSM + (g + 1) * N_STATE]
            c = xc[t:t + 1, D_SSM + (G_SSM + g) * N_STATE:D_SSM + (G_SSM + g + 1) * N_STATE]
            s = s * da[rows, t:t + 1] + xw[rows, t:t + 1] * b
            y_sc[rows, t:t + 1] = jnp.sum(s * c, axis=1, keepdims=True)
        sto_ref[0, 2 * pi:2 * pi + 2] = s.reshape(2, HD_SSM, N_STATE)
    y = (y_sc[...] + dskt_ref[...] * xct[0:D_SSM, :]) * _silu(zt_ref[0])
    half = D_SSM // G_SSM
    parts = []
    for g in range(G_SSM):
        yg = y[g * half:(g + 1) * half, :]
        parts.append(yg * lax.rsqrt(jnp.mean(yg * yg, axis=0, keepdims=True) + EPS))
    yt_ref[0] = jnp.concatenate(parts, axis=0) * gt_ref[...]


def _dec_ssm(layer, xsh, xsht, cw, cwt, cb, cbt, dt_s, alog, zt, dskt, gt, state_ssm):
    per_seq = lambda a: pl.BlockSpec((1,) + a.shape[1:], lambda b: (b,) + (0,) * (a.ndim - 1))
    full = lambda a: pl.BlockSpec(a.shape, lambda b: (0,) * a.ndim)
    return pl.pallas_call(
        _dec_ssm_kernel,
        grid=(DEC_BATCH,),
        in_specs=[per_seq(xsh), per_seq(xsht), full(cw), full(cwt), full(cb), full(cbt), per_seq(dt_s),
                  full(alog), per_seq(zt), full(dskt), full(gt),
                  pl.BlockSpec((1, 1, H_SSM, HD_SSM, N_STATE), lambda b: (layer, b, 0, 0, 0))],
        out_specs=(pl.BlockSpec((1, H_SSM, HD_SSM, N_STATE), lambda b: (b, 0, 0, 0)),
                   pl.BlockSpec((1, D_SSM, DEC_SEQ), lambda b: (b, 0, 0))),
        out_shape=(jax.ShapeDtypeStruct((DEC_BATCH, H_SSM, HD_SSM, N_STATE), F32),
                   jax.ShapeDtypeStruct((DEC_BATCH, D_SSM, DEC_SEQ), F32)),
        scratch_shapes=[pltpu.VMEM((D_SSM, DEC_SEQ), F32)],
        compiler_params=pltpu.CompilerParams(dimension_semantics=("arbitrary",),
                                             vmem_limit_bytes=VMEM_LIMIT),
        name="dec_ssm",
    )(xsh, xsht, cw, cwt, cb, cbt, dt_s, alog, zt, dskt, gt, state_ssm)


def _small_row(*pieces):
    v = jnp.concatenate([p.astype(F32) for p in pieces])
    return jnp.pad(v, (0, SMALL_W - v.shape[0]))[None, :]


def kernel(x_prompt, x_sample, cache_k, cache_v, cache_logf, state_ssm, state_conv, page_table, meta,
           w_in, b_f, dt_bias, a_log, d_skip, conv_w, conv_b, g_pre_mix, g_post_mix, g_att_out, g_ssm_out,
           w_out, g_pre_ffn, g_post_ffn, w_gate, w_up, w_down):
    assert x_prompt.shape == (BATCH, SEQ, D_MODEL) and x_sample.shape == (DEC_BATCH, DEC_SEQ, D_MODEL)
    assert page_table.shape == (DEC_BATCH, N_PAGES)
    n_pool = cache_k.shape[1]

    xp = jnp.concatenate([jnp.broadcast_to(meta.astype(F32)[None], (BATCH, N_META, D_MODEL)), x_prompt,
                          jnp.zeros((BATCH, LP - L_VALID, D_MODEL), F32)], axis=1)
    x = xp.reshape(M_ROWS, D_MODEL)
    x = lax.dynamic_update_slice(x, x_sample.reshape(N_SAMPLE, D_MODEL), (SAMPLE_ROW0, 0))

    page_flat = page_table.reshape(-1).astype(jnp.int32)
    lfp = cache_logf.reshape(DEPTH, n_pool, LF_ROWS, LANES)
    hmask = jnp.where((jnp.arange(NROW_DEC)[:, None] % H_ATT) == (jnp.arange(PAGE_ROWS)[None, :] % H_ATT),
                      0.0, NEG_INF).astype(F32)

    srows = slice(SAMPLE_ROW0, SAMPLE_ROW0 + N_SAMPLE)

    assert w_in.shape == (DEPTH, D_MODEL, D_IN)
    w_in_bf, w_out_bf, w_gate_bf, w_up_bf, w_down_bf = (w.astype(BF16) for w in (w_in, w_out, w_gate, w_up, w_down))

    e_head, e_chunk = _head_expansion(HD_SSM), _head_expansion(Q_SSD)

    outs = {k: [] for k in ("k_p", "v_p", "lf_p", "k_s", "v_s", "lf_s", "ssm_p", "ssm_s", "conv_p", "conv_s")}
    for i in range(DEPTH):
        bias = _small_row(b_f[i], dt_bias[i])
        q_bf, k_bf, vt_bf, k_p, v_p, k_s, v_s, z, xbc, small, fcs = _inproj(
            i, x, g_pre_mix[i][None, :], w_in_bf, bias)

        att = _attn_prompt(q_bf, k_bf, vt_bf, fcs)
        alog_row = jnp.pad(a_log[i].astype(F32), (DT_LANE0, SMALL_W - DT_LANE0 - H_SSM))[None, :]
        dsk_row = jnp.repeat(d_skip[i].astype(F32), HD_SSM)[None, :]
        g_ssm_row = g_ssm_out[i].astype(F32)[None, :]
        y_ssm, st_p = _ssd_prompt(xbc, small, z, conv_w[i], conv_b[i][None, :], alog_row, dsk_row, g_ssm_row,
                                  e_head, e_chunk)

        q_s = q_bf[srows].reshape(DEC_BATCH, NROW_DEC, HD_ATT)
        kn = k_s.reshape(DEC_BATCH, NROW_DEC, HD_ATT)
        vn = v_s.reshape(DEC_BATCH, NROW_DEC, HD_ATT)
        lf_s = small[srows, :H_ATT].reshape(DEC_BATCH, DEC_SEQ, H_ATT)
        att_s = _dec_attn(i, page_flat, q_s, kn, vn, lf_s.reshape(DEC_BATCH, NROW_DEC, 1),
                          lf_s.reshape(DEC_BATCH, 1, NROW_DEC), hmask, cache_k, cache_v, lfp[i])
        xbc_s = xbc[srows].reshape(DEC_BATCH, DEC_SEQ, C_CONV)
        xpad = jnp.concatenate([state_conv[i].astype(F32), xbc_s], axis=1)
        xsh = jnp.stack([xpad[:, j:j + DEC_SEQ] for j in range(CONV_W)], axis=1)
        xsht = xsh.transpose(0, 1, 3, 2)
        dt_s = small[srows, DT_LANE0:DT_LANE0 + H_SSM].reshape(DEC_BATCH, DEC_SEQ, H_SSM)
        dt_rows = jnp.repeat(dt_s, HD_SSM, axis=2).transpose(0, 2, 1)
        alog_col = jnp.repeat(a_log[i].astype(F32), HD_SSM)[:, None]
        zt = z[srows].reshape(DEC_BATCH, DEC_SEQ, D_SSM).transpose(0, 2, 1)
        st_s, y_s_t = _dec_ssm(i, xsh, xsht, conv_w[i], conv_w[i].T, conv_b[i][None, :], conv_b[i][:, None],
                               dt_rows, alog_col, zt, dsk_row.T, g_ssm_row.T, state_ssm)

        x1, h2 = _outproj(i, att, y_ssm, att_s.reshape(N_SAMPLE, D_ATT),
                          y_s_t.transpose(0, 2, 1).reshape(N_SAMPLE, D_SSM), x, w_out_bf,
                          g_att_out[i][None, :], g_post_mix[i][None, :], g_pre_ffn[i][None, :])
        x = _ffn(i, h2, w_gate_bf, w_up_bf, w_down_bf, x1, g_post_ffn[i][None, :])

        outs["k_p"].append(k_p.reshape(BATCH, L_VALID, H_ATT, HD_ATT))
        outs["v_p"].append(v_p.reshape(BATCH, L_VALID, H_ATT, HD_ATT))
        outs["lf_p"].append(small[:, :H_ATT].reshape(BATCH, LP, H_ATT)[:, :L_VALID])
        outs["k_s"].append(kn.reshape(DEC_BATCH, DEC_SEQ, H_ATT, HD_ATT))
        outs["v_s"].append(vn.reshape(DEC_BATCH, DEC_SEQ, H_ATT, HD_ATT))
        outs["lf_s"].append(lf_s)
        outs["ssm_p"].append(st_p)
        outs["ssm_s"].append(st_s)
        outs["conv_p"].append(xbc.reshape(BATCH, LP, C_CONV)[:, L_VALID - (CONV_W - 1):L_VALID])
        outs["conv_s"].append(xpad[:, DEC_SEQ:])

    xf = x.reshape(BATCH, LP, D_MODEL)
    y_prompt = xf[:, N_META:L_VALID]
    y_sample = x[srows].reshape(DEC_BATCH, DEC_SEQ, D_MODEL)
    st = lambda k: jnp.stack(outs[k])
    return (y_prompt, y_sample, st("k_p"), st("v_p"), st("lf_p"), st("k_s"), st("v_s"), st("lf_s"),
            st("ssm_p"), st("ssm_s"), st("conv_p"), st("conv_s"))
```

```python
import functools
import math

import jax
import jax.numpy as jnp
from jax import lax
from jax.experimental import pallas as pl
from jax.experimental.pallas import tpu as pltpu

D_MODEL = 2048
BATCH = 2
SEQ = 4096
DEPTH = 4
DEC_BATCH = 8
DEC_SEQ = 4
PAST_LEN = 16384
PAGE_SIZE = 128
N_META = 16
H_ATT = 8
HD_ATT = 128
D_ATT = H_ATT * HD_ATT
H_SSM = 16
HD_SSM = 64
D_SSM = H_SSM * HD_SSM
G_SSM = 2
R_SSM = H_SSM // G_SSM
N_STATE = 128
CONV_W = 4
C_CONV = D_SSM + 2 * G_SSM * N_STATE
D_MIX = D_ATT + D_SSM
D_FF = ((8 * D_MODEL + 3 * 256 - 1) // (3 * 256)) * 256
EPS = 1e-6
ATT_SCALE = HD_ATT ** -0.5

LANES = 128
SUBLANES = 8
L_VALID = N_META + SEQ
LP = 4352
M_ROWS = BATCH * LP
N_SAMPLE = DEC_BATCH * DEC_SEQ
SAMPLE_LOCAL = 4128
SAMPLE_ROW0 = (BATCH - 1) * LP + SAMPLE_LOCAL
N_PAGES = PAST_LEN // PAGE_SIZE
SMALL_W = LANES
D_IN = 3 * D_ATT + H_ATT + D_SSM + C_CONV + H_SSM
DT_LANE0 = H_ATT

TM_PROJ = 256
TM_FFN = 512
TF_FFN = 512
TQ = 512
TK = 512
TK_DIAG = 256
HEADS_PER_STEP = 2
NQ_FULL = L_VALID // TQ - 1
TQ_LAST = -(-(L_VALID - NQ_FULL * TQ) // LANES) * LANES
ATT_ROWS = NQ_FULL * TQ + TQ_LAST
LOG2E = math.log2(math.e)
Q_SSD = 128
VMEM_LIMIT = 56 * 1024 * 1024

F32 = jnp.float32
BF16 = jnp.bfloat16
NEG_INF = float("-inf")


def _nt_dot(a, b):
    return lax.dot_general(a, b, (((1,), (1,)), ((), ())), preferred_element_type=F32)


def _silu(x):
    return x * (1.0 / (1.0 + jnp.exp(-x)))


def _rms(x, g):
    return x * lax.rsqrt(jnp.mean(x * x, axis=-1, keepdims=True) + EPS) * g


def _cumsum_rows(x):
    n = x.shape[0]
    row = lax.broadcasted_iota(jnp.int32, x.shape, 0)
    s = 1
    while s < n:
        x = x + jnp.where(row >= s, pltpu.roll(x, s, 0), 0.0)
        s *= 2
    return x


def _inproj_kernel(x_ref, g_ref, w_ref, b_ref, q_ref, kb_ref, vt_ref, kp_ref, vp_ref, ks_ref, vs_ref,
                   z_ref, xbc_ref, sm_ref, fcs_ref, carry_sc):
    i = pl.program_id(0)
    h = _rms(x_ref[...], g_ref[...]).astype(BF16)

    def mm(c0, c1):
        return jnp.dot(h, w_ref[0, :, c0:c1], preferred_element_type=F32)

    q_ref[...] = ((mm(0, D_ATT) * ATT_SCALE) * LOG2E).astype(BF16)
    k = mm(D_ATT, 2 * D_ATT)
    kp_ref[0] = k
    kb_ref[...] = k.astype(BF16)
    v = mm(2 * D_ATT, 3 * D_ATT)
    vp_ref[0] = v
    vt_ref[...] = v.T.astype(BF16)

    @pl.when(i == SAMPLE_ROW0 // TM_PROJ)
    def _():
        r0 = SAMPLE_ROW0 % TM_PROJ
        ks_ref[...] = k[r0:r0 + N_SAMPLE]
        vs_ref[...] = v[r0:r0 + N_SAMPLE]

    rest = mm(3 * D_ATT, D_IN)
    z_ref[...] = rest[:, H_ATT:H_ATT + D_SSM]
    xbc_ref[...] = rest[:, H_ATT + D_SSM:H_ATT + D_SSM + C_CONV]
    lane = lax.broadcasted_iota(jnp.int32, (TM_PROJ, SMALL_W), 1)
    tail0 = (H_ATT + D_SSM + C_CONV) // LANES * LANES
    last = jnp.concatenate([rest[:, tail0:], jnp.zeros((TM_PROJ, SMALL_W - (D_IN - 3 * D_ATT - tail0)), F32)],
                           axis=1)
    u = jnp.where(lane < H_ATT, rest[:, 0:SMALL_W], jnp.where(lane < H_ATT + H_SSM, last, 0.0)) + b_ref[...]
    t = jnp.log(1.0 + jnp.exp(-jnp.abs(u)))
    sm = jnp.where(lane < H_ATT, jnp.minimum(u, 0.0) - t, jnp.maximum(u, 0.0) + t)
    sm_ref[...] = sm

    @pl.when(i % (LP // TM_PROJ) == 0)
    def _():
        carry_sc[...] = jnp.zeros_like(carry_sc)

    cs = _cumsum_rows(sm) + carry_sc[...]
    fcs_ref[...] = cs
    carry_sc[...] = cs[TM_PROJ - 1:TM_PROJ, :]


def _inproj(layer, x, g, w, bias):
    n = M_ROWS // TM_PROJ
    row = lambda w_: pl.BlockSpec((TM_PROJ, w_), lambda i: (i, 0))
    full = lambda a: pl.BlockSpec(a.shape, lambda i: (0,) * a.ndim)
    tiles_per_seq = LP // TM_PROJ
    cache = pl.BlockSpec((1, TM_PROJ, D_ATT), lambda i: (i // tiles_per_seq, i % tiles_per_seq, 0))
    sample = pl.BlockSpec((N_SAMPLE, D_ATT), lambda i: (0, 0))
    outs = (
        (jax.ShapeDtypeStruct((M_ROWS, D_ATT), BF16), row(D_ATT)),
        (jax.ShapeDtypeStruct((M_ROWS, D_ATT), BF16), row(D_ATT)),
        (jax.ShapeDtypeStruct((D_ATT, M_ROWS), BF16),
         pl.BlockSpec((D_ATT, TM_PROJ), lambda i: (0, i))),
        (jax.ShapeDtypeStruct((BATCH, L_VALID, D_ATT), F32), cache),
        (jax.ShapeDtypeStruct((BATCH, L_VALID, D_ATT), F32), cache),
        (jax.ShapeDtypeStruct((N_SAMPLE, D_ATT), F32), sample),
        (jax.ShapeDtypeStruct((N_SAMPLE, D_ATT), F32), sample),
        (jax.ShapeDtypeStruct((M_ROWS, D_SSM), F32), row(D_SSM)),
        (jax.ShapeDtypeStruct((M_ROWS, C_CONV), F32), row(C_CONV)),
        (jax.ShapeDtypeStruct((M_ROWS, SMALL_W), F32), row(SMALL_W)),
        (jax.ShapeDtypeStruct((M_ROWS, SMALL_W), F32), row(SMALL_W)),
    )
    return pl.pallas_call(
        _inproj_kernel,
        grid=(n,),
        in_specs=[row(D_MODEL), full(g),
                  pl.BlockSpec((1,) + w.shape[1:], lambda i: (layer, 0, 0), pipeline_mode=pl.Buffered(1)),
                  full(bias)],
        out_specs=tuple(o[1] for o in outs),
        out_shape=tuple(o[0] for o in outs),
        scratch_shapes=[pltpu.VMEM((1, SMALL_W), F32)],
        compiler_params=pltpu.CompilerParams(dimension_semantics=("arbitrary",),
                                             vmem_limit_bytes=VMEM_LIMIT),
        name="inproj",
    )(x, g, w, bias)


def _attn_step(hh, ks, tk, q0, lo, w, masked, q_ref, k_ref, fkb_sc, vt_ref, m_sc, l_sc, acc_sc):
    s = _attn_scores(hh, ks, tk, q0 + lo, w, q_ref, k_ref)
    _attn_accumulate(hh, s, ks, tk, lo, w, masked, fkb_sc, vt_ref, m_sc, l_sc, acc_sc)


def _attn_scores(hh, ks, tk, qs, w, q_ref, k_ref):
    hs = slice(hh * HD_ATT, (hh + 1) * HD_ATT)
    return _nt_dot(k_ref[pl.ds(ks, tk), hs], q_ref[pl.ds(qs, w), hs])


def _attn_accumulate(hh, s, ks, tk, lo, w, masked, fkb_sc, vt_ref, m_sc, l_sc, acc_sc):
    s = s - jnp.concatenate([fkb_sc[hh, pl.ds(ks, tk), :]] * (w // LANES), axis=1)
    if masked:
        s = jnp.where(lax.broadcasted_iota(jnp.int32, (tk, w), 0) <= lax.broadcasted_iota(jnp.int32, (tk, w), 1),
                      s, NEG_INF)
    m_prev = m_sc[hh, :, lo:lo + w]
    m_new = jnp.maximum(m_prev, jnp.max(s, axis=0, keepdims=True))
    alpha = jnp.exp2(m_prev - m_new)
    p = jnp.exp2(s - m_new)
    l_sc[hh, :, lo:lo + w] = alpha * l_sc[hh, :, lo:lo + w] + jnp.sum(p, axis=0, keepdims=True)
    acc_sc[hh, :, lo:lo + w] = alpha * acc_sc[hh, :, lo:lo + w] + jnp.dot(
        vt_ref[hh * HD_ATT:(hh + 1) * HD_ATT, pl.ds(ks, tk)], p.astype(BF16), preferred_element_type=F32)
    m_sc[hh, :, lo:lo + w] = m_new


def _attn_kernel(q_ref, k_ref, vt_ref, fcs_ref, o_ref, fkb_sc, s_sc, m_sc, l_sc, acc_sc):
    hp = pl.program_id(1)
    lane = lax.broadcasted_iota(jnp.int32, (TK_DIAG, LANES), 1)
    heads = range(HEADS_PER_STEP)

    def prep(c, carry):
        r0 = pl.multiple_of(c * TK_DIAG, TK_DIAG)
        f = fcs_ref[pl.ds(r0, TK_DIAG), :]
        for hh in heads:
            col = jnp.sum(jnp.where(lane == hp * HEADS_PER_STEP + hh, f, 0.0), axis=1, keepdims=True) * LOG2E
            fkb_sc[hh, pl.ds(r0, TK_DIAG), :] = jnp.broadcast_to(col, (TK_DIAG, LANES))
        return carry

    lax.fori_loop(0, LP // TK_DIAG, prep, 0)
    acc_refs = (fkb_sc, vt_ref, m_sc, l_sc, acc_sc)

    def q_tile(q0, w, n_unmasked):
        n = jnp.asarray(n_unmasked, jnp.int32)
        for hh in heads:
            m_sc[hh, :, 0:w] = jnp.full((1, w), NEG_INF, F32)
            l_sc[hh, :, 0:w] = jnp.zeros((1, w), F32)
            acc_sc[hh, :, 0:w] = jnp.zeros((HD_ATT, w), F32)

        def scores_to(slot, j):
            for hh in heads:
                s_sc[slot, hh, :, 0:w] = _attn_scores(hh, pl.multiple_of(j * TK, TK), TK, q0, w, q_ref, k_ref)

        def accumulate_from(slot, j):
            for hh in heads:
                _attn_accumulate(hh, s_sc[slot, hh, :, 0:w], pl.multiple_of(j * TK, TK), TK, 0, w, False,
                                 *acc_refs)

        @pl.when(n > 0)
        def _():
            scores_to(0, 0)

        def body(jj, carry):
            scores_to(1, 2 * jj + 1)
            accumulate_from(0, 2 * jj)
            scores_to(0, 2 * jj + 2)
            accumulate_from(1, 2 * jj + 1)
            return carry

        n_pairs = jnp.maximum(n - 1, 0) // 2
        lax.fori_loop(0, n_pairs, body, 0)

        @pl.when((n > 0) & ((n & 1) == 1))
        def _():
            accumulate_from(0, 2 * n_pairs)

        @pl.when((n > 0) & ((n & 1) == 0))
        def _():
            scores_to(1, 2 * n_pairs + 1)
            accumulate_from(0, 2 * n_pairs)
            accumulate_from(1, 2 * n_pairs + 1)

        for lo in range(0, w, TK_DIAG):
            for hh in heads:
                _attn_step(hh, q0 + lo, TK_DIAG, q0, lo, w - lo, True, q_ref, k_ref, *acc_refs)
        for hh in heads:
            o_ref[pl.ds(q0, w), hh * HD_ATT:(hh + 1) * HD_ATT] = (
                acc_sc[hh, :, 0:w] * (1.0 / l_sc[hh, :, 0:w])).T

    def full_tile(i, carry):
        q_tile(pl.multiple_of(i * TQ, TQ), TQ, i * (TQ // TK))
        return carry

    lax.fori_loop(0, NQ_FULL, full_tile, 0)
    q_tile(NQ_FULL * TQ, TQ_LAST, NQ_FULL * TQ // TK)
    o_ref[ATT_ROWS:LP, :] = jnp.zeros((LP - ATT_ROWS, HEADS_PER_STEP * HD_ATT), F32)


def _attn_prompt(q_bf, k_bf, vt_bf, fcs):
    blk = pl.BlockSpec((LP, HEADS_PER_STEP * HD_ATT), lambda b, h: (b, h))
    return pl.pallas_call(
        _attn_kernel,
        grid=(BATCH, H_ATT // HEADS_PER_STEP),
        in_specs=[blk, blk, pl.BlockSpec((HEADS_PER_STEP * HD_ATT, LP), lambda b, h: (h, b)),
                  pl.BlockSpec((LP, SMALL_W), lambda b, h: (b, 0))],
        out_specs=blk,
        out_shape=jax.ShapeDtypeStruct((M_ROWS, D_ATT), F32),
        scratch_shapes=[pltpu.VMEM((HEADS_PER_STEP, LP, LANES), F32),
                        pltpu.VMEM((2, HEADS_PER_STEP, TK, TQ_LAST), F32),
                        pltpu.VMEM((HEADS_PER_STEP, 1, TQ_LAST), F32),
                        pltpu.VMEM((HEADS_PER_STEP, 1, TQ_LAST), F32),
                        pltpu.VMEM((HEADS_PER_STEP, HD_ATT, TQ_LAST), F32)],
        compiler_params=pltpu.CompilerParams(dimension_semantics=("arbitrary", "arbitrary"),
                                             vmem_limit_bytes=VMEM_LIMIT),
        name="attn_prompt",
    )(q_bf, k_bf, vt_bf, fcs)


PAIR_W = 2 * HD_SSM


def _expand(v, e_ref):
    hi = v.astype(BF16)
    r1 = v - hi.astype(F32)
    mid = r1.astype(BF16)
    lo = (r1 - mid.astype(F32)).astype(BF16)
    e = e_ref[...]
    return (jnp.dot(hi, e, preferred_element_type=F32) + jnp.dot(mid, e, preferred_element_type=F32)) + jnp.dot(
        lo, e, preferred_element_type=F32)


def _head_expansion(width):
    r = jnp.arange(SMALL_W)[:, None]
    c = jnp.arange(H_SSM * width)[None, :]
    return (r == DT_LANE0 + c // width).astype(BF16)


def _ssd_kernel(xbc_ref, sm_ref, z_ref, cw_ref, cb_ref, alog_ref, dsk_ref, g_ref, e_ref, e2_ref, y_ref, st_ref,
                tail_sc, state_sc):
    c = pl.program_id(1)
    nc = pl.num_programs(1)
    Q = Q_SSD

    @pl.when(c == 0)
    def _():
        tail_sc[...] = jnp.zeros_like(tail_sc)
        state_sc[...] = jnp.zeros_like(state_sc)

    x = xbc_ref[...]
    pt = tail_sc[...]
    row8 = lax.broadcasted_iota(jnp.int32, (SUBLANES, C_CONV), 0)
    acc = x * cw_ref[CONV_W - 1:CONV_W, :]
    for k in range(1, CONV_W):
        r = pltpu.roll(x, k, 0)
        first = jnp.where(row8 < k, pltpu.roll(pt, k, 0), r[0:SUBLANES])
        shifted = jnp.concatenate([first, r[SUBLANES:]], axis=0)
        acc = acc + shifted * cw_ref[CONV_W - 1 - k:CONV_W - k, :]
    tail_sc[...] = x[Q - SUBLANES:Q]
    xc = _silu(acc + cb_ref[...])
    xs = xc[:, :D_SSM]

    pos = c * Q + lax.broadcasted_iota(jnp.int32, (Q, LANES), 0)
    lane = lax.broadcasted_iota(jnp.int32, (Q, LANES), 1)
    is_dt = (pos < L_VALID) & (lane >= DT_LANE0) & (lane < DT_LANE0 + H_SSM)
    dt = jnp.where(is_dt, sm_ref[...], 0.0)
    a_dt = dt * (-jnp.exp(alog_ref[...]))
    acs = _cumsum_rows(a_dt)
    acs_t = acs.T
    last = acs[Q - 1:Q, :]
    chunk_decay = jnp.exp(last)
    x_dt = xs * _expand(dt, e_ref)
    xd_t = (x_dt * _expand(jnp.exp(last - acs), e_ref)).T
    dout_x = _expand(jnp.exp(acs), e_ref)
    acs_x = _expand(acs, e2_ref)
    tril = (lax.broadcasted_iota(jnp.int32, (Q, Q), 0) >= lax.broadcasted_iota(jnp.int32, (Q, Q), 1))
    low = lane < HD_SSM
    rowi = lax.broadcasted_iota(jnp.int32, (PAIR_W, N_STATE), 0)

    ys = []
    for g in range(G_SSM):
        bg = xc[:, D_SSM + g * N_STATE:D_SSM + (g + 1) * N_STATE].astype(BF16)
        cg = xc[:, D_SSM + (G_SSM + g) * N_STATE:D_SSM + (G_SSM + g + 1) * N_STATE].astype(BF16)
        cb = _nt_dot(cg, bg)
        for pr in range(R_SSM // 2):
            h0 = g * R_SSM + 2 * pr
            pi = h0 // 2
            ps = slice(pi * PAIR_W, (pi + 1) * PAIR_W)
            ms = []
            for hh in (h0, h0 + 1):
                col = DT_LANE0 + hh
                seg = jnp.exp(jnp.where(tril, acs_x[:, hh * Q:(hh + 1) * Q] - acs_t[col:col + 1, :], NEG_INF))
                ms.append((cb * seg).astype(BF16))
            xp = x_dt[:, ps]
            rhs = jnp.concatenate([jnp.where(low, xp, 0.0), jnp.where(low, 0.0, xp)], axis=0).astype(BF16)
            y_diag = jnp.dot(jnp.concatenate(ms, axis=1), rhs, preferred_element_type=F32)
            st = state_sc[pi]
            y_off = _nt_dot(cg, st.astype(BF16)) * dout_x[:, ps]
            ys.append(y_diag + y_off)
            new = jnp.dot(xd_t[ps, :].astype(BF16), bg, preferred_element_type=F32)
            c0 = DT_LANE0 + h0
            cd = jnp.where(rowi < HD_SSM, chunk_decay[:, c0:c0 + 1], chunk_decay[:, c0 + 1:c0 + 2])
            state_sc[pi] = st * cd + new

    y = (jnp.concatenate(ys, axis=1) + dsk_ref[...] * xs) * _silu(z_ref[...])
    half = D_SSM // G_SSM
    parts = []
    for g in range(G_SSM):
        yg = y[:, g * half:(g + 1) * half]
        parts.append(yg * lax.rsqrt(jnp.mean(yg * yg, axis=-1, keepdims=True) + EPS))
    y_ref[...] = jnp.concatenate(parts, axis=1) * g_ref[...]

    @pl.when(c == nc - 1)
    def _():
        st_ref[0] = state_sc[...].reshape(H_SSM, HD_SSM, N_STATE)


def _ssd_prompt(xbc, small, z, conv_w, conv_b, alog_row, dsk_row, g_row, e1, e2):
    nc = LP // Q_SSD
    row = lambda w_: pl.BlockSpec((Q_SSD, w_), lambda b, c: (b * nc + c, 0))
    full = lambda a: pl.BlockSpec(a.shape, lambda b, c: (0,) * a.ndim)
    return pl.pallas_call(
        _ssd_kernel,
        grid=(BATCH, nc),
        in_specs=[row(C_CONV), row(SMALL_W), row(D_SSM), full(conv_w), full(conv_b), full(alog_row),
                  full(dsk_row), full(g_row), full(e1), full(e2)],
        out_specs=(row(D_SSM),
                   pl.BlockSpec((1, H_SSM, HD_SSM, N_STATE), lambda b, c: (b, 0, 0, 0))),
        out_shape=(jax.ShapeDtypeStruct((M_ROWS, D_SSM), F32),
                   jax.ShapeDtypeStruct((BATCH, H_SSM, HD_SSM, N_STATE), F32)),
        scratch_shapes=[pltpu.VMEM((SUBLANES, C_CONV), F32),
                        pltpu.VMEM((H_SSM // 2, PAIR_W, N_STATE), F32)],
        compiler_params=pltpu.CompilerParams(dimension_semantics=("arbitrary", "arbitrary"),
                                             vmem_limit_bytes=VMEM_LIMIT),
        name="ssd_prompt",
    )(xbc, small, z, conv_w, conv_b, alog_row, dsk_row, g_row, e1, e2)


def _outproj_kernel(att_ref, y_ref, atts_ref, ys_ref, x_ref, w_ref, ga_ref, gm_ref, gf_ref, x1_ref, h2_ref):
    r0 = SAMPLE_ROW0 % TM_PROJ
    rows = pl.program_id(0) * TM_PROJ + lax.broadcasted_iota(jnp.int32, (TM_PROJ, 1), 0)
    is_sample = (rows >= SAMPLE_ROW0) & (rows < SAMPLE_ROW0 + N_SAMPLE)

    def merged(main, small):
        placed = jnp.concatenate([jnp.zeros((r0, small.shape[1]), F32), small,
                                  jnp.zeros((TM_PROJ - r0 - N_SAMPLE, small.shape[1]), F32)], axis=0)
        return jnp.where(is_sample, placed, main)

    a = _rms(merged(att_ref[...], atts_ref[...]), ga_ref[...]).astype(BF16)
    y = merged(y_ref[...], ys_ref[...])
    mixed = jnp.dot(a, w_ref[0, 0:D_ATT, :], preferred_element_type=F32)
    mixed = mixed + jnp.dot(y.astype(BF16), w_ref[0, D_ATT:D_MIX, :], preferred_element_type=F32)
    x1 = x_ref[...] + _rms(mixed, gm_ref[...])
    x1_ref[...] = x1
    h2_ref[...] = _rms(x1, gf_ref[...]).astype(BF16)


def _outproj(layer, att, y, att_s, y_s, x, w, g_att, g_post_mix, g_pre_ffn):
    n = M_ROWS // TM_PROJ
    row = lambda w_: pl.BlockSpec((TM_PROJ, w_), lambda i: (i, 0))
    full = lambda a: pl.BlockSpec(a.shape, lambda i: (0,) * a.ndim)
    return pl.pallas_call(
        _outproj_kernel,
        grid=(n,),
        in_specs=[row(D_ATT), row(D_SSM), full(att_s), full(y_s), row(D_MODEL),
                  pl.BlockSpec((1,) + w.shape[1:], lambda i: (layer, 0, 0), pipeline_mode=pl.Buffered(1)),
                  full(g_att), full(g_post_mix), full(g_pre_ffn)],
        out_specs=(row(D_MODEL), row(D_MODEL)),
        out_shape=(jax.ShapeDtypeStruct((M_ROWS, D_MODEL), F32),
                   jax.ShapeDtypeStruct((M_ROWS, D_MODEL), BF16)),
        compiler_params=pltpu.CompilerParams(dimension_semantics=("arbitrary",),
                                             vmem_limit_bytes=VMEM_LIMIT),
        name="outproj",
    )(att, y, att_s, y_s, x, w, g_att, g_post_mix, g_pre_ffn)


def _ffn_kernel(h_ref, wg_ref, wu_ref, wd_ref, x1_ref, g_ref, o_ref):
    j = pl.program_id(1)

    @pl.when(j == 0)
    def _():
        o_ref[...] = jnp.zeros_like(o_ref)

    h = h_ref[...]
    gate = jnp.dot(h, wg_ref[0], preferred_element_type=F32)
    up = jnp.dot(h, wu_ref[0], preferred_element_type=F32)
    act = (_silu(gate) * up).astype(BF16)
    o_ref[...] += jnp.dot(act, wd_ref[0], preferred_element_type=F32)

    @pl.when(j == pl.num_programs(1) - 1)
    def _():
        o_ref[...] = x1_ref[...] + _rms(o_ref[...], g_ref[...])


def _ffn(layer, h2, wg, wu, wd, x1, g_post_ffn):
    return pl.pallas_call(
        _ffn_kernel,
        grid=(M_ROWS // TM_FFN, D_FF // TF_FFN),
        in_specs=[pl.BlockSpec((TM_FFN, D_MODEL), lambda i, j: (i, 0)),
                  pl.BlockSpec((1, D_MODEL, TF_FFN), lambda i, j: (layer, 0, j)),
                  pl.BlockSpec((1, D_MODEL, TF_FFN), lambda i, j: (layer, 0, j)),
                  pl.BlockSpec((1, TF_FFN, D_MODEL), lambda i, j: (layer, j, 0)),
                  pl.BlockSpec((TM_FFN, D_MODEL), lambda i, j: (i, 0)),
                  pl.BlockSpec(g_post_ffn.shape, lambda i, j: (0, 0))],
        out_specs=pl.BlockSpec((TM_FFN, D_MODEL), lambda i, j: (i, 0)),
        out_shape=jax.ShapeDtypeStruct((M_ROWS, D_MODEL), F32),
        compiler_params=pltpu.CompilerParams(dimension_semantics=("arbitrary", "arbitrary"),
                                             vmem_limit_bytes=VMEM_LIMIT),
        name="ffn",
    )(h2, wg, wu, wd, x1, g_post_ffn)


NROW_DEC = DEC_SEQ * H_ATT
PAGE_ROWS = PAGE_SIZE * H_ATT
LF_ROWS = PAGE_ROWS // LANES
PAGES_PER_STEP = 8


def _dec_attn_kernel(pt_ref, q_ref, kn_ref, vn_ref, lfc_ref, lfr_ref, hmask_ref, *rest):
    del pt_ref
    n = PAGES_PER_STEP
    kp_refs, vp_refs, lfp_refs = rest[0:n], rest[n:2 * n], rest[2 * n:3 * n]
    o_ref, m_sc, l_sc, acc_sc, carry_sc, cn_sc = rest[3 * n:]
    j = pl.program_id(1)
    q = q_ref[0]

    def update(s, v_bf):
        m_prev = m_sc[...]
        m_new = jnp.maximum(m_prev, jnp.max(s, axis=1, keepdims=True))
        alpha = jnp.exp2(m_prev - m_new)
        p = jnp.exp2(s - m_new)
        l_sc[...] = alpha * l_sc[...] + jnp.sum(p, axis=1, keepdims=True)
        acc_sc[...] = alpha * acc_sc[...] + jnp.dot(p.astype(BF16), v_bf, preferred_element_type=F32)
        m_sc[...] = m_new

    @pl.when(j == 0)
    def _():
        r = lax.broadcasted_iota(jnp.int32, (NROW_DEC, NROW_DEC), 0)
        c = lax.broadcasted_iota(jnp.int32, (NROW_DEC, NROW_DEC), 1)
        same = (r & (H_ATT - 1)) == (c & (H_ATT - 1))
        keep = same & ((c >> 3) <= (r >> 3))
        cn_col = jnp.sum(jnp.where(keep, lfr_ref[0] * LOG2E, 0.0), axis=1, keepdims=True)
        cn_row = jnp.sum(jnp.where(same & ((r >> 3) <= (c >> 3)), lfc_ref[0] * LOG2E, 0.0),
                         axis=0, keepdims=True)
        m_sc[...] = jnp.full_like(m_sc, NEG_INF)
        l_sc[...] = jnp.zeros_like(l_sc)
        acc_sc[...] = jnp.zeros_like(acc_sc)
        carry_sc[...] = jnp.zeros_like(carry_sc)
        cn_sc[...] = cn_col
        s = _nt_dot(q, kn_ref[0].astype(BF16)) + cn_col - cn_row
        update(jnp.where(keep, s, NEG_INF), vn_ref[0].astype(BF16))

    lane = lax.broadcasted_iota(jnp.int32, (LF_ROWS, LANES), 1)
    rowi = lax.broadcasted_iota(jnp.int32, (LF_ROWS, LANES), 0)
    carry = carry_sc[...]
    base = cn_sc[...] + hmask_ref[...]
    scores = []
    for i in range(PAGES_PER_STEP):
        lf = lfp_refs[i][0] * LOG2E
        suf = lf
        tot = lf
        s_ = H_ATT
        while s_ < LANES:
            suf = suf + jnp.where(lane < LANES - s_, pltpu.roll(suf, LANES - s_, 1), 0.0)
            tot = tot + pltpu.roll(tot, s_, 1)
            s_ *= 2
        rsuf = tot
        s_ = 1
        while s_ < LF_ROWS:
            rsuf = rsuf + jnp.where(rowi < LF_ROWS - s_, pltpu.roll(rsuf, LF_ROWS - s_, 0), 0.0)
            s_ *= 2
        decay = carry + (suf - lf) + (rsuf - tot)
        carry = carry + rsuf[0:1, :]
        bias = jnp.concatenate([decay[r:r + 1, :] for r in range(LF_ROWS)], axis=1)
        k = kp_refs[i][0, 0].reshape(PAGE_ROWS, HD_ATT).astype(BF16)
        scores.append(_nt_dot(q, k) + (base + bias))
    carry_sc[...] = carry

    m_prev = m_sc[...]
    m_new = m_prev
    for s in scores:
        m_new = jnp.maximum(m_new, jnp.max(s, axis=1, keepdims=True))
    alpha = jnp.exp2(m_prev - m_new)
    l_new = alpha * l_sc[...]
    acc = alpha * acc_sc[...]
    for i, s in enumerate(scores):
        p = jnp.exp2(s - m_new)
        l_new = l_new + jnp.sum(p, axis=1, keepdims=True)
        v = vp_refs[i][0, 0].reshape(PAGE_ROWS, HD_ATT).astype(BF16)
        acc = acc + jnp.dot(p.astype(BF16), v, preferred_element_type=F32)
    l_sc[...] = l_new
    acc_sc[...] = acc
    m_sc[...] = m_new

    @pl.when(j == pl.num_programs(1) - 1)
    def _():
        o_ref[0] = acc_sc[...] * (1.0 / l_sc[...])


def _dec_attn(layer, page_flat, q_s, kn, vn, lf_col, lf_row, hmask, cache_k, cache_v, lfp):
    def page(i):
        return lambda b, j, pt: pt[b * N_PAGES + (N_PAGES - 1 - (j * PAGES_PER_STEP + i))]

    def kv_spec(i):
        pg = page(i)
        return pl.BlockSpec((1, 1, PAGE_SIZE, H_ATT, HD_ATT), lambda b, j, pt: (layer, pg(b, j, pt), 0, 0, 0))

    def lf_spec(i):
        pg = page(i)
        return pl.BlockSpec((1, LF_ROWS, LANES), lambda b, j, pt: (pg(b, j, pt), 0, 0))

    per_seq = lambda a: pl.BlockSpec((1,) + a.shape[1:], lambda b, j, pt: (b,) + (0,) * (a.ndim - 1))
    steps = range(PAGES_PER_STEP)
    grid_spec = pltpu.PrefetchScalarGridSpec(
        num_scalar_prefetch=1,
        grid=(DEC_BATCH, N_PAGES // PAGES_PER_STEP),
        in_specs=[per_seq(q_s), per_seq(kn), per_seq(vn), per_seq(lf_col), per_seq(lf_row),
                  pl.BlockSpec(hmask.shape, lambda b, j, pt: (0, 0))]
                 + [kv_spec(i) for i in steps] + [kv_spec(i) for i in steps] + [lf_spec(i) for i in steps],
        out_specs=pl.BlockSpec((1, NROW_DEC, HD_ATT), lambda b, j, pt: (b, 0, 0)),
        scratch_shapes=[pltpu.VMEM((NROW_DEC, 1), F32), pltpu.VMEM((NROW_DEC, 1), F32),
                        pltpu.VMEM((NROW_DEC, HD_ATT), F32), pltpu.VMEM((1, LANES), F32),
                        pltpu.VMEM((NROW_DEC, 1), F32)],
    )
    return pl.pallas_call(
        _dec_attn_kernel,
        grid_spec=grid_spec,
        out_shape=jax.ShapeDtypeStruct((DEC_BATCH, NROW_DEC, HD_ATT), F32),
        compiler_params=pltpu.CompilerParams(dimension_semantics=("arbitrary", "arbitrary"),
                                             vmem_limit_bytes=VMEM_LIMIT),
        name="dec_attn",
    )(page_flat, q_s, kn, vn, lf_col, lf_row, hmask, *([cache_k] * PAGES_PER_STEP),
      *([cache_v] * PAGES_PER_STEP), *([lfp] * PAGES_PER_STEP))


BC_W = 2 * G_SSM * N_STATE


def _dec_ssm_kernel(xsh_ref, bct_ref, cw_ref, cwt_ref, cb_ref, cbt_ref, dt_ref, alog_ref, z_ref,
                    dsk_ref, g_ref, st_ref, sto_ref, y_ref, y_sc):
    xc = cb_ref[...]
    bct = cbt_ref[...]
    for j in range(CONV_W):
        xc = xc + xsh_ref[0, j] * cw_ref[j:j + 1, :]
        bct = bct + bct_ref[0, j] * cwt_ref[:, j:j + 1]
    xs = _silu(xc)[:, 0:D_SSM]
    bct = _silu(bct)
    dtx = dt_ref[0]
    da = jnp.exp(dtx * (-jnp.exp(alog_ref[...])))
    xw = xs * dtx
    cols = {}
    for g in range(G_SSM):
        for t in range(DEC_SEQ):
            b0 = g * N_STATE
            c0 = (G_SSM + g) * N_STATE
            cols[g, t] = (jnp.broadcast_to(bct[b0:b0 + N_STATE, t:t + 1], (N_STATE, PAIR_W)),
                          jnp.broadcast_to(bct[c0:c0 + N_STATE, t:t + 1], (N_STATE, PAIR_W)))
    for pi in range(H_SSM // 2):
        g = (2 * pi) // R_SSM
        lanes = slice(pi * PAIR_W, (pi + 1) * PAIR_W)
        st = st_ref[0, 0, 2 * pi:2 * pi + 2].reshape(PAIR_W, N_STATE).T
        for t in range(DEC_SEQ):
            bb, cc = cols[g, t]
            st = st * da[t:t + 1, lanes] + bb * xw[t:t + 1, lanes]
            y_sc[t:t + 1, lanes] = jnp.sum(st * cc, axis=0, keepdims=True)
        sto_ref[0, 2 * pi:2 * pi + 2] = st.T.reshape(2, HD_SSM, N_STATE)
    y = (y_sc[...] + dsk_ref[...] * xs) * _silu(z_ref[0])
    half = D_SSM // G_SSM
    parts = []
    for g in range(G_SSM):
        yg = y[:, g * half:(g + 1) * half]
        parts.append(yg * lax.rsqrt(jnp.mean(yg * yg, axis=-1, keepdims=True) + EPS))
    y_ref[0] = jnp.concatenate(parts, axis=1) * g_ref[...]


def _dec_ssm(layer, xsh, bct, cw, cwt, cb, cbt, dt_x, alog_row, z_s, dsk_row, g_row, state_ssm):
    per_seq = lambda a: pl.BlockSpec((1,) + a.shape[1:], lambda b: (b,) + (0,) * (a.ndim - 1))
    full = lambda a: pl.BlockSpec(a.shape, lambda b: (0,) * a.ndim)
    return pl.pallas_call(
        _dec_ssm_kernel,
        grid=(DEC_BATCH,),
        in_specs=[per_seq(xsh), per_seq(bct), full(cw), full(cwt), full(cb), full(cbt), per_seq(dt_x),
                  full(alog_row), per_seq(z_s), full(dsk_row), full(g_row),
                  pl.BlockSpec((1, 1, H_SSM, HD_SSM, N_STATE), lambda b: (layer, b, 0, 0, 0))],
        out_specs=(pl.BlockSpec((1, H_SSM, HD_SSM, N_STATE), lambda b: (b, 0, 0, 0)),
                   pl.BlockSpec((1, DEC_SEQ, D_SSM), lambda b: (b, 0, 0))),
        out_shape=(jax.ShapeDtypeStruct((DEC_BATCH, H_SSM, HD_SSM, N_STATE), F32),
                   jax.ShapeDtypeStruct((DEC_BATCH, DEC_SEQ, D_SSM), F32)),
        scratch_shapes=[pltpu.VMEM((DEC_SEQ, D_SSM), F32)],
        compiler_params=pltpu.CompilerParams(dimension_semantics=("arbitrary",),
                                             vmem_limit_bytes=VMEM_LIMIT),
        name="dec_ssm",
    )(xsh, bct, cw, cwt, cb, cbt, dt_x, alog_row, z_s, dsk_row, g_row, state_ssm)


def _small_row(*pieces):
    v = jnp.concatenate([p.astype(F32) for p in pieces])
    return jnp.pad(v, (0, SMALL_W - v.shape[0]))[None, :]


def kernel(x_prompt, x_sample, cache_k, cache_v, cache_logf, state_ssm, state_conv, page_table, meta,
           w_in, b_f, dt_bias, a_log, d_skip, conv_w, conv_b, g_pre_mix, g_post_mix, g_att_out, g_ssm_out,
           w_out, g_pre_ffn, g_post_ffn, w_gate, w_up, w_down):
    assert x_prompt.shape == (BATCH, SEQ, D_MODEL) and x_sample.shape == (DEC_BATCH, DEC_SEQ, D_MODEL)
    assert page_table.shape == (DEC_BATCH, N_PAGES)
    n_pool = cache_k.shape[1]

    xp = jnp.concatenate([jnp.broadcast_to(meta.astype(F32)[None], (BATCH, N_META, D_MODEL)), x_prompt,
                          jnp.zeros((BATCH, LP - L_VALID, D_MODEL), F32)], axis=1)
    x = xp.reshape(M_ROWS, D_MODEL)
    x = lax.dynamic_update_slice(x, x_sample.reshape(N_SAMPLE, D_MODEL), (SAMPLE_ROW0, 0))

    page_flat = page_table.reshape(-1).astype(jnp.int32)
    lfp = cache_logf.reshape(DEPTH, n_pool, LF_ROWS, LANES)
    hmask = jnp.where((jnp.arange(NROW_DEC)[:, None] % H_ATT) == (jnp.arange(PAGE_ROWS)[None, :] % H_ATT),
                      0.0, NEG_INF).astype(F32)

    srows = slice(SAMPLE_ROW0, SAMPLE_ROW0 + N_SAMPLE)

    assert w_in.shape == (DEPTH, D_MODEL, D_IN)
    w_in_bf, w_out_bf, w_gate_bf, w_up_bf, w_down_bf = (w.astype(BF16) for w in (w_in, w_out, w_gate, w_up, w_down))

    e_head, e_chunk = _head_expansion(HD_SSM), _head_expansion(Q_SSD)

    outs = {k: [] for k in ("k_p", "v_p", "lf_p", "k_s", "v_s", "lf_s", "ssm_p", "ssm_s", "conv_p", "conv_s")}
    for i in range(DEPTH):
        bias = _small_row(b_f[i], dt_bias[i])
        q_bf, k_bf, vt_bf, k_p, v_p, k_s, v_s, z, xbc, small, fcs = _inproj(
            i, x, g_pre_mix[i][None, :], w_in_bf, bias)

        att = _attn_prompt(q_bf, k_bf, vt_bf, fcs)
        alog_row = jnp.pad(a_log[i].astype(F32), (DT_LANE0, SMALL_W - DT_LANE0 - H_SSM))[None, :]
        dsk_row = jnp.repeat(d_skip[i].astype(F32), HD_SSM)[None, :]
        g_ssm_row = g_ssm_out[i].astype(F32)[None, :]
        y_ssm, st_p = _ssd_prompt(xbc, small, z, conv_w[i], conv_b[i][None, :], alog_row, dsk_row, g_ssm_row,
                                  e_head, e_chunk)

        q_s = q_bf[srows].reshape(DEC_BATCH, NROW_DEC, HD_ATT)
        kn = k_s.reshape(DEC_BATCH, NROW_DEC, HD_ATT)
        vn = v_s.reshape(DEC_BATCH, NROW_DEC, HD_ATT)
        lf_s = small[srows, :H_ATT].reshape(DEC_BATCH, DEC_SEQ, H_ATT)
        att_s = _dec_attn(i, page_flat, q_s, kn, vn, lf_s.reshape(DEC_BATCH, NROW_DEC, 1),
                          lf_s.reshape(DEC_BATCH, 1, NROW_DEC), hmask, cache_k, cache_v, lfp[i])
        xbc_s = xbc[srows].reshape(DEC_BATCH, DEC_SEQ, C_CONV)
        xpad = jnp.concatenate([state_conv[i].astype(F32), xbc_s], axis=1)
        xsh = jnp.stack([xpad[:, j:j + DEC_SEQ] for j in range(CONV_W)], axis=1)
        bct = xsh[..., D_SSM:].transpose(0, 1, 3, 2)
        dt_s = small[srows, DT_LANE0:DT_LANE0 + H_SSM].reshape(DEC_BATCH, DEC_SEQ, H_SSM)
        dt_x = jnp.repeat(dt_s, HD_SSM, axis=2)
        alog_x = jnp.repeat(a_log[i].astype(F32), HD_SSM)[None, :]
        z_s = z[srows].reshape(DEC_BATCH, DEC_SEQ, D_SSM)
        st_s, y_s = _dec_ssm(i, xsh, bct, conv_w[i], conv_w[i][:, D_SSM:].T, conv_b[i][None, :],
                             conv_b[i][D_SSM:, None], dt_x, alog_x, z_s, dsk_row, g_ssm_row, state_ssm)

        x1, h2 = _outproj(i, att, y_ssm, att_s.reshape(N_SAMPLE, D_ATT), y_s.reshape(N_SAMPLE, D_SSM), x,
                          w_out_bf, g_att_out[i][None, :], g_post_mix[i][None, :], g_pre_ffn[i][None, :])
        x = _ffn(i, h2, w_gate_bf, w_up_bf, w_down_bf, x1, g_post_ffn[i][None, :])

        outs["k_p"].append(k_p.reshape(BATCH, L_VALID, H_ATT, HD_ATT))
        outs["v_p"].append(v_p.reshape(BATCH, L_VALID, H_ATT, HD_ATT))
        outs["lf_p"].append(small[:, :H_ATT].reshape(BATCH, LP, H_ATT)[:, :L_VALID])
        outs["k_s"].append(kn.reshape(DEC_BATCH, DEC_SEQ, H_ATT, HD_ATT))
        outs["v_s"].append(vn.reshape(DEC_BATCH, DEC_SEQ, H_ATT, HD_ATT))
        outs["lf_s"].append(lf_s)
        outs["ssm_p"].append(st_p)
        outs["ssm_s"].append(st_s)
        outs["conv_p"].append(xbc.reshape(BATCH, LP, C_CONV)[:, L_VALID - (CONV_W - 1):L_VALID])
        outs["conv_s"].append(xpad[:, DEC_SEQ:])

    xf = x.reshape(BATCH, LP, D_MODEL)
    y_prompt = xf[:, N_META:L_VALID]
    y_sample = x[srows].reshape(DEC_BATCH, DEC_SEQ, D_MODEL)
    st = lambda k: jnp.stack(outs[k])
    return (y_prompt, y_sample, st("k_p"), st("v_p"), st("lf_p"), st("k_s"), st("v_s"), st("lf_s"),
            st("ssm_p"), st("ssm_s"), st("conv_p"), st("conv_s"))
```

```python
import functools
import math

import jax
import jax.numpy as jnp
from jax import lax
from jax.experimental import pallas as pl
from jax.experimental.pallas import tpu as pltpu

D_MODEL = 2048
BATCH = 2
SEQ = 4096
DEPTH = 4
DEC_BATCH = 8
DEC_SEQ = 4
PAST_LEN = 16384
PAGE_SIZE = 128
N_META = 16
H_ATT = 8
HD_ATT = 128
D_ATT = H_ATT * HD_ATT
H_SSM = 16
HD_SSM = 64
D_SSM = H_SSM * HD_SSM
G_SSM = 2
R_SSM = H_SSM // G_SSM
N_STATE = 128
CONV_W = 4
C_CONV = D_SSM + 2 * G_SSM * N_STATE
D_MIX = D_ATT + D_SSM
D_FF = ((8 * D_MODEL + 3 * 256 - 1) // (3 * 256)) * 256
EPS = 1e-6
ATT_SCALE = HD_ATT ** -0.5

LANES = 128
SUBLANES = 8
L_VALID = N_META + SEQ
LP = 4352
M_ROWS = BATCH * LP
N_SAMPLE = DEC_BATCH * DEC_SEQ
SAMPLE_LOCAL = 4128
SAMPLE_ROW0 = (BATCH - 1) * LP + SAMPLE_LOCAL
N_PAGES = PAST_LEN // PAGE_SIZE
SMALL_W = LANES
D_IN = 3 * D_ATT + H_ATT + D_SSM + C_CONV + H_SSM
DT_LANE0 = H_ATT

TM_PROJ = 256
TM_FFN = 512
TF_FFN = 512
TQ = 512
TK = 512
TK_DIAG = 256
HEADS_PER_STEP = 2
NQ_FULL = L_VALID // TQ - 1
TQ_LAST = -(-(L_VALID - NQ_FULL * TQ) // LANES) * LANES
ATT_ROWS = NQ_FULL * TQ + TQ_LAST
LOG2E = math.log2(math.e)
Q_SSD = 128
VMEM_LIMIT = 56 * 1024 * 1024

F32 = jnp.float32
BF16 = jnp.bfloat16
NEG_INF = float("-inf")


def _nt_dot(a, b):
    return lax.dot_general(a, b, (((1,), (1,)), ((), ())), preferred_element_type=F32)


def _silu(x):
    return x * (1.0 / (1.0 + jnp.exp(-x)))


def _rms(x, g):
    return x * lax.rsqrt(jnp.mean(x * x, axis=-1, keepdims=True) + EPS) * g


def _cumsum_rows(x):
    n = x.shape[0]
    row = lax.broadcasted_iota(jnp.int32, x.shape, 0)
    s = 1
    while s < n:
        x = x + jnp.where(row >= s, pltpu.roll(x, s, 0), 0.0)
        s *= 2
    return x


def _inproj_kernel(x_ref, g_ref, w_ref, b_ref, q_ref, kb_ref, vt_ref, kp_ref, vp_ref, ks_ref, vs_ref,
                   z_ref, xbc_ref, sm_ref, fcs_ref, carry_sc):
    i = pl.program_id(0)
    h = _rms(x_ref[...], g_ref[...]).astype(BF16)

    def mm(c0, c1):
        return jnp.dot(h, w_ref[0, :, c0:c1], preferred_element_type=F32)

    q_ref[...] = ((mm(0, D_ATT) * ATT_SCALE) * LOG2E).astype(BF16)
    k = mm(D_ATT, 2 * D_ATT)
    kp_ref[0] = k
    kb_ref[...] = k.astype(BF16)
    v = mm(2 * D_ATT, 3 * D_ATT)
    vp_ref[0] = v
    vt_ref[...] = v.T.astype(BF16)

    @pl.when(i == SAMPLE_ROW0 // TM_PROJ)
    def _():
        r0 = SAMPLE_ROW0 % TM_PROJ
        ks_ref[...] = k[r0:r0 + N_SAMPLE]
        vs_ref[...] = v[r0:r0 + N_SAMPLE]

    rest = mm(3 * D_ATT, D_IN)
    z_ref[...] = rest[:, H_ATT:H_ATT + D_SSM]
    xbc_ref[...] = rest[:, H_ATT + D_SSM:H_ATT + D_SSM + C_CONV]
    lane = lax.broadcasted_iota(jnp.int32, (TM_PROJ, SMALL_W), 1)
    tail0 = (H_ATT + D_SSM + C_CONV) // LANES * LANES
    last = jnp.concatenate([rest[:, tail0:], jnp.zeros((TM_PROJ, SMALL_W - (D_IN - 3 * D_ATT - tail0)), F32)],
                           axis=1)
    u = jnp.where(lane < H_ATT, rest[:, 0:SMALL_W], jnp.where(lane < H_ATT + H_SSM, last, 0.0)) + b_ref[...]
    t = jnp.log(1.0 + jnp.exp(-jnp.abs(u)))
    sm = jnp.where(lane < H_ATT, jnp.minimum(u, 0.0) - t, jnp.maximum(u, 0.0) + t)
    sm_ref[...] = sm

    @pl.when(i % (LP // TM_PROJ) == 0)
    def _():
        carry_sc[...] = jnp.zeros_like(carry_sc)

    cs = _cumsum_rows(sm) + carry_sc[...]
    fcs_ref[...] = cs
    carry_sc[...] = cs[TM_PROJ - 1:TM_PROJ, :]


def _inproj(layer, x, g, w, bias):
    n = M_ROWS // TM_PROJ
    row = lambda w_: pl.BlockSpec((TM_PROJ, w_), lambda i: (i, 0))
    full = lambda a: pl.BlockSpec(a.shape, lambda i: (0,) * a.ndim)
    tiles_per_seq = LP // TM_PROJ
    cache = pl.BlockSpec((1, TM_PROJ, D_ATT), lambda i: (i // tiles_per_seq, i % tiles_per_seq, 0))
    sample = pl.BlockSpec((N_SAMPLE, D_ATT), lambda i: (0, 0))
    outs = (
        (jax.ShapeDtypeStruct((M_ROWS, D_ATT), BF16), row(D_ATT)),
        (jax.ShapeDtypeStruct((M_ROWS, D_ATT), BF16), row(D_ATT)),
        (jax.ShapeDtypeStruct((D_ATT, M_ROWS), BF16),
         pl.BlockSpec((D_ATT, TM_PROJ), lambda i: (0, i))),
        (jax.ShapeDtypeStruct((BATCH, L_VALID, D_ATT), F32), cache),
        (jax.ShapeDtypeStruct((BATCH, L_VALID, D_ATT), F32), cache),
        (jax.ShapeDtypeStruct((N_SAMPLE, D_ATT), F32), sample),
        (jax.ShapeDtypeStruct((N_SAMPLE, D_ATT), F32), sample),
        (jax.ShapeDtypeStruct((M_ROWS, D_SSM), F32), row(D_SSM)),
        (jax.ShapeDtypeStruct((M_ROWS, C_CONV), F32), row(C_CONV)),
        (jax.ShapeDtypeStruct((M_ROWS, SMALL_W), F32), row(SMALL_W)),
        (jax.ShapeDtypeStruct((M_ROWS, SMALL_W), F32), row(SMALL_W)),
    )
    return pl.pallas_call(
        _inproj_kernel,
        grid=(n,),
        in_specs=[row(D_MODEL), full(g),
                  pl.BlockSpec((1,) + w.shape[1:], lambda i: (layer, 0, 0), pipeline_mode=pl.Buffered(1)),
                  full(bias)],
        out_specs=tuple(o[1] for o in outs),
        out_shape=tuple(o[0] for o in outs),
        scratch_shapes=[pltpu.VMEM((1, SMALL_W), F32)],
        compiler_params=pltpu.CompilerParams(dimension_semantics=("arbitrary",),
                                             vmem_limit_bytes=VMEM_LIMIT),
        name="inproj",
    )(x, g, w, bias)


def _attn_step(hh, ks, tk, q0, lo, w, masked, q_ref, k_ref, fkb_sc, vt_ref, m_sc, l_sc, acc_sc):
    s = _attn_scores(hh, ks, tk, q0 + lo, w, q_ref, k_ref)
    _attn_accumulate(hh, s, ks, tk, lo, w, masked, fkb_sc, vt_ref, m_sc, l_sc, acc_sc)


def _attn_scores(hh, ks, tk, qs, w, q_ref, k_ref):
    hs = slice(hh * HD_ATT, (hh + 1) * HD_ATT)
    return _nt_dot(k_ref[pl.ds(ks, tk), hs], q_ref[pl.ds(qs, w), hs])


def _attn_accumulate(hh, s, ks, tk, lo, w, masked, fkb_sc, vt_ref, m_sc, l_sc, acc_sc):
    s = s - jnp.concatenate([fkb_sc[hh, pl.ds(ks, tk), :]] * (w // LANES), axis=1)
    if masked:
        s = jnp.where(lax.broadcasted_iota(jnp.int32, (tk, w), 0) <= lax.broadcasted_iota(jnp.int32, (tk, w), 1),
                      s, NEG_INF)
    m_prev = m_sc[hh, :, lo:lo + w]
    m_new = jnp.maximum(m_prev, jnp.max(s, axis=0, keepdims=True))
    alpha = jnp.exp2(m_prev - m_new)
    p = jnp.exp2(s - m_new)
    l_sc[hh, :, lo:lo + w] = alpha * l_sc[hh, :, lo:lo + w] + jnp.sum(p, axis=0, keepdims=True)
    acc_sc[hh, :, lo:lo + w] = alpha * acc_sc[hh, :, lo:lo + w] + jnp.dot(
        vt_ref[hh * HD_ATT:(hh + 1) * HD_ATT, pl.ds(ks, tk)], p.astype(BF16), preferred_element_type=F32)
    m_sc[hh, :, lo:lo + w] = m_new


def _attn_kernel(q_ref, k_ref, vt_ref, fcs_ref, o_ref, fkb_sc, s_sc, m_sc, l_sc, acc_sc):
    hp = pl.program_id(1)
    lane = lax.broadcasted_iota(jnp.int32, (TK_DIAG, LANES), 1)
    heads = range(HEADS_PER_STEP)

    def prep(c, carry):
        r0 = pl.multiple_of(c * TK_DIAG, TK_DIAG)
        f = fcs_ref[pl.ds(r0, TK_DIAG), :]
        for hh in heads:
            col = jnp.sum(jnp.where(lane == hp * HEADS_PER_STEP + hh, f, 0.0), axis=1, keepdims=True) * LOG2E
            fkb_sc[hh, pl.ds(r0, TK_DIAG), :] = jnp.broadcast_to(col, (TK_DIAG, LANES))
        return carry

    lax.fori_loop(0, LP // TK_DIAG, prep, 0)
    acc_refs = (fkb_sc, vt_ref, m_sc, l_sc, acc_sc)

    def q_tile(q0, w, n_unmasked):
        n = jnp.asarray(n_unmasked, jnp.int32)
        for hh in heads:
            m_sc[hh, :, 0:w] = jnp.full((1, w), NEG_INF, F32)
            l_sc[hh, :, 0:w] = jnp.zeros((1, w), F32)
            acc_sc[hh, :, 0:w] = jnp.zeros((HD_ATT, w), F32)

        def scores_to(slot, j):
            for hh in heads:
                s_sc[slot, hh, :, 0:w] = _attn_scores(hh, pl.multiple_of(j * TK, TK), TK, q0, w, q_ref, k_ref)

        def accumulate_from(slot, j):
            for hh in heads:
                _attn_accumulate(hh, s_sc[slot, hh, :, 0:w], pl.multiple_of(j * TK, TK), TK, 0, w, False,
                                 *acc_refs)

        @pl.when(n > 0)
        def _():
            scores_to(0, 0)

        def body(jj, carry):
            scores_to(1, 2 * jj + 1)
            accumulate_from(0, 2 * jj)
            scores_to(0, 2 * jj + 2)
            accumulate_from(1, 2 * jj + 1)
            return carry

        n_pairs = jnp.maximum(n - 1, 0) // 2
        lax.fori_loop(0, n_pairs, body, 0)

        @pl.when((n > 0) & ((n & 1) == 1))
        def _():
            accumulate_from(0, 2 * n_pairs)

        @pl.when((n > 0) & ((n & 1) == 0))
        def _():
            scores_to(1, 2 * n_pairs + 1)
            accumulate_from(0, 2 * n_pairs)
            accumulate_from(1, 2 * n_pairs + 1)

        for lo in range(0, w, TK_DIAG):
            for hh in heads:
                _attn_step(hh, q0 + lo, TK_DIAG, q0, lo, w - lo, True, q_ref, k_ref, *acc_refs)
        for hh in heads:
            o_ref[pl.ds(q0, w), hh * HD_ATT:(hh + 1) * HD_ATT] = (
                acc_sc[hh, :, 0:w] * (1.0 / l_sc[hh, :, 0:w])).T

    def full_tile(i, carry):
        q_tile(pl.multiple_of(i * TQ, TQ), TQ, i * (TQ // TK))
        return carry

    lax.fori_loop(0, NQ_FULL, full_tile, 0)
    q_tile(NQ_FULL * TQ, TQ_LAST, NQ_FULL * TQ // TK)
    o_ref[ATT_ROWS:LP, :] = jnp.zeros((LP - ATT_ROWS, HEADS_PER_STEP * HD_ATT), F32)


def _attn_prompt(q_bf, k_bf, vt_bf, fcs):
    blk = pl.BlockSpec((LP, HEADS_PER_STEP * HD_ATT), lambda b, h: (b, h))
    return pl.pallas_call(
        _attn_kernel,
        grid=(BATCH, H_ATT // HEADS_PER_STEP),
        in_specs=[blk, blk, pl.BlockSpec((HEADS_PER_STEP * HD_ATT, LP), lambda b, h: (h, b)),
                  pl.BlockSpec((LP, SMALL_W), lambda b, h: (b, 0))],
        out_specs=blk,
        out_shape=jax.ShapeDtypeStruct((M_ROWS, D_ATT), F32),
        scratch_shapes=[pltpu.VMEM((HEADS_PER_STEP, LP, LANES), F32),
                        pltpu.VMEM((2, HEADS_PER_STEP, TK, TQ_LAST), F32),
                        pltpu.VMEM((HEADS_PER_STEP, 1, TQ_LAST), F32),
                        pltpu.VMEM((HEADS_PER_STEP, 1, TQ_LAST), F32),
                        pltpu.VMEM((HEADS_PER_STEP, HD_ATT, TQ_LAST), F32)],
        compiler_params=pltpu.CompilerParams(dimension_semantics=("arbitrary", "arbitrary"),
                                             vmem_limit_bytes=VMEM_LIMIT),
        name="attn_prompt",
    )(q_bf, k_bf, vt_bf, fcs)


PAIR_W = 2 * HD_SSM


def _expand(v, e_ref):
    hi = v.astype(BF16)
    r1 = v - hi.astype(F32)
    mid = r1.astype(BF16)
    lo = (r1 - mid.astype(F32)).astype(BF16)
    e = e_ref[...]
    return (jnp.dot(hi, e, preferred_element_type=F32) + jnp.dot(mid, e, preferred_element_type=F32)) + jnp.dot(
        lo, e, preferred_element_type=F32)


def _head_expansion(width):
    r = jnp.arange(SMALL_W)[:, None]
    c = jnp.arange(H_SSM * width)[None, :]
    return (r == DT_LANE0 + c // width).astype(BF16)


def _ssd_kernel(xbc_ref, sm_ref, z_ref, cw_ref, cb_ref, alog_ref, dsk_ref, g_ref, e_ref, e2_ref, y_ref, st_ref,
                tail_sc, state_sc):
    c = pl.program_id(1)
    nc = pl.num_programs(1)
    Q = Q_SSD

    @pl.when(c == 0)
    def _():
        tail_sc[...] = jnp.zeros_like(tail_sc)
        state_sc[...] = jnp.zeros_like(state_sc)

    x = xbc_ref[...]
    pt = tail_sc[...]
    row8 = lax.broadcasted_iota(jnp.int32, (SUBLANES, C_CONV), 0)
    acc = x * cw_ref[CONV_W - 1:CONV_W, :]
    for k in range(1, CONV_W):
        r = pltpu.roll(x, k, 0)
        first = jnp.where(row8 < k, pltpu.roll(pt, k, 0), r[0:SUBLANES])
        shifted = jnp.concatenate([first, r[SUBLANES:]], axis=0)
        acc = acc + shifted * cw_ref[CONV_W - 1 - k:CONV_W - k, :]
    tail_sc[...] = x[Q - SUBLANES:Q]
    xc = _silu(acc + cb_ref[...])
    xs = xc[:, :D_SSM]

    pos = c * Q + lax.broadcasted_iota(jnp.int32, (Q, LANES), 0)
    lane = lax.broadcasted_iota(jnp.int32, (Q, LANES), 1)
    is_dt = (pos < L_VALID) & (lane >= DT_LANE0) & (lane < DT_LANE0 + H_SSM)
    dt = jnp.where(is_dt, sm_ref[...], 0.0)
    a_dt = dt * (-jnp.exp(alog_ref[...]))
    acs = _cumsum_rows(a_dt)
    acs_t = acs.T
    last = acs[Q - 1:Q, :]
    chunk_decay = jnp.exp(last)
    x_dt = xs * _expand(dt, e_ref)
    xd_t = (x_dt * _expand(jnp.exp(last - acs), e_ref)).T
    dout_x = _expand(jnp.exp(acs), e_ref)
    acs_x = _expand(acs, e2_ref)
    tril = (lax.broadcasted_iota(jnp.int32, (Q, Q), 0) >= lax.broadcasted_iota(jnp.int32, (Q, Q), 1))
    low = lane < HD_SSM
    rowi = lax.broadcasted_iota(jnp.int32, (PAIR_W, N_STATE), 0)

    ys = []
    for g in range(G_SSM):
        bg = xc[:, D_SSM + g * N_STATE:D_SSM + (g + 1) * N_STATE].astype(BF16)
        cg = xc[:, D_SSM + (G_SSM + g) * N_STATE:D_SSM + (G_SSM + g + 1) * N_STATE].astype(BF16)
        cb = _nt_dot(cg, bg)
        for pr in range(R_SSM // 2):
            h0 = g * R_SSM + 2 * pr
            pi = h0 // 2
            ps = slice(pi * PAIR_W, (pi + 1) * PAIR_W)
            ms = []
            for hh in (h0, h0 + 1):
                col = DT_LANE0 + hh
                seg = jnp.exp(jnp.where(tril, acs_x[:, hh * Q:(hh + 1) * Q] - acs_t[col:col + 1, :], NEG_INF))
                ms.append((cb * seg).astype(BF16))
            xp = x_dt[:, ps]
            rhs = jnp.concatenate([jnp.where(low, xp, 0.0), jnp.where(low, 0.0, xp)], axis=0).astype(BF16)
            y_diag = jnp.dot(jnp.concatenate(ms, axis=1), rhs, preferred_element_type=F32)
            st = state_sc[pi]
            y_off = _nt_dot(cg, st.astype(BF16)) * dout_x[:, ps]
            ys.append(y_diag + y_off)
            new = jnp.dot(xd_t[ps, :].astype(BF16), bg, preferred_element_type=F32)
            c0 = DT_LANE0 + h0
            cd = jnp.where(rowi < HD_SSM, chunk_decay[:, c0:c0 + 1], chunk_decay[:, c0 + 1:c0 + 2])
            state_sc[pi] = st * cd + new

    y = (jnp.concatenate(ys, axis=1) + dsk_ref[...] * xs) * _silu(z_ref[...])
    half = D_SSM // G_SSM
    parts = []
    for g in range(G_SSM):
        yg = y[:, g * half:(g + 1) * half]
        parts.append(yg * lax.rsqrt(jnp.mean(yg * yg, axis=-1, keepdims=True) + EPS))
    y_ref[...] = jnp.concatenate(parts, axis=1) * g_ref[...]

    @pl.when(c == nc - 1)
    def _():
        st_ref[0] = state_sc[...].reshape(H_SSM, HD_SSM, N_STATE)


def _ssd_prompt(xbc, small, z, conv_w, conv_b, alog_row, dsk_row, g_row, e1, e2):
    nc = LP // Q_SSD
    row = lambda w_: pl.BlockSpec((Q_SSD, w_), lambda b, c: (b * nc + c, 0))
    full = lambda a: pl.BlockSpec(a.shape, lambda b, c: (0,) * a.ndim)
    return pl.pallas_call(
        _ssd_kernel,
        grid=(BATCH, nc),
        in_specs=[row(C_CONV), row(SMALL_W), row(D_SSM), full(conv_w), full(conv_b), full(alog_row),
                  full(dsk_row), full(g_row), full(e1), full(e2)],
        out_specs=(row(D_SSM),
                   pl.BlockSpec((1, H_SSM, HD_SSM, N_STATE), lambda b, c: (b, 0, 0, 0))),
        out_shape=(jax.ShapeDtypeStruct((M_ROWS, D_SSM), F32),
                   jax.ShapeDtypeStruct((BATCH, H_SSM, HD_SSM, N_STATE), F32)),
        scratch_shapes=[pltpu.VMEM((SUBLANES, C_CONV), F32),
                        pltpu.VMEM((H_SSM // 2, PAIR_W, N_STATE), F32)],
        compiler_params=pltpu.CompilerParams(dimension_semantics=("arbitrary", "arbitrary"),
                                             vmem_limit_bytes=VMEM_LIMIT),
        name="ssd_prompt",
    )(xbc, small, z, conv_w, conv_b, alog_row, dsk_row, g_row, e1, e2)


def _outproj_kernel(att_ref, y_ref, atts_ref, ys_ref, x_ref, w_ref, ga_ref, gm_ref, gf_ref, x1_ref, h2_ref):
    r0 = SAMPLE_ROW0 % TM_PROJ
    rows = pl.program_id(0) * TM_PROJ + lax.broadcasted_iota(jnp.int32, (TM_PROJ, 1), 0)
    is_sample = (rows >= SAMPLE_ROW0) & (rows < SAMPLE_ROW0 + N_SAMPLE)

    def merged(main, small):
        placed = jnp.concatenate([jnp.zeros((r0, small.shape[1]), F32), small,
                                  jnp.zeros((TM_PROJ - r0 - N_SAMPLE, small.shape[1]), F32)], axis=0)
        return jnp.where(is_sample, placed, main)

    a = _rms(merged(att_ref[...], atts_ref[...]), ga_ref[...]).astype(BF16)
    y = merged(y_ref[...], ys_ref[...])
    mixed = jnp.dot(a, w_ref[0, 0:D_ATT, :], preferred_element_type=F32)
    mixed = mixed + jnp.dot(y.astype(BF16), w_ref[0, D_ATT:D_MIX, :], preferred_element_type=F32)
    x1 = x_ref[...] + _rms(mixed, gm_ref[...])
    x1_ref[...] = x1
    h2_ref[...] = _rms(x1, gf_ref[...]).astype(BF16)


def _outproj(layer, att, y, att_s, y_s, x, w, g_att, g_post_mix, g_pre_ffn):
    n = M_ROWS // TM_PROJ
    row = lambda w_: pl.BlockSpec((TM_PROJ, w_), lambda i: (i, 0))
    full = lambda a: pl.BlockSpec(a.shape, lambda i: (0,) * a.ndim)
    return pl.pallas_call(
        _outproj_kernel,
        grid=(n,),
        in_specs=[row(D_ATT), row(D_SSM), full(att_s), full(y_s), row(D_MODEL),
                  pl.BlockSpec((1,) + w.shape[1:], lambda i: (layer, 0, 0), pipeline_mode=pl.Buffered(1)),
                  full(g_att), full(g_post_mix), full(g_pre_ffn)],
        out_specs=(row(D_MODEL), row(D_MODEL)),
        out_shape=(jax.ShapeDtypeStruct((M_ROWS, D_MODEL), F32),
                   jax.ShapeDtypeStruct((M_ROWS, D_MODEL), BF16)),
        compiler_params=pltpu.CompilerParams(dimension_semantics=("arbitrary",),
                                             vmem_limit_bytes=VMEM_LIMIT),
        name="outproj",
    )(att, y, att_s, y_s, x, w, g_att, g_post_mix, g_pre_ffn)


def _ffn_kernel(h_ref, wg_ref, wu_ref, wd_ref, x1_ref, g_ref, o_ref):
    j = pl.program_id(1)

    @pl.when(j == 0)
    def _():
        o_ref[...] = jnp.zeros_like(o_ref)

    h = h_ref[...]
    gate = jnp.dot(h, wg_ref[0], preferred_element_type=F32)
    up = jnp.dot(h, wu_ref[0], preferred_element_type=F32)
    act = (_silu(gate) * up).astype(BF16)
    o_ref[...] += jnp.dot(act, wd_ref[0], preferred_element_type=F32)

    @pl.when(j == pl.num_programs(1) - 1)
    def _():
        o_ref[...] = x1_ref[...] + _rms(o_ref[...], g_ref[...])


def _ffn(layer, h2, wg, wu, wd, x1, g_post_ffn):
    return pl.pallas_call(
        _ffn_kernel,
        grid=(M_ROWS // TM_FFN, D_FF // TF_FFN),
        in_specs=[pl.BlockSpec((TM_FFN, D_MODEL), lambda i, j: (i, 0)),
                  pl.BlockSpec((1, D_MODEL, TF_FFN), lambda i, j: (layer, 0, j)),
                  pl.BlockSpec((1, D_MODEL, TF_FFN), lambda i, j: (layer, 0, j)),
                  pl.BlockSpec((1, TF_FFN, D_MODEL), lambda i, j: (layer, j, 0)),
                  pl.BlockSpec((TM_FFN, D_MODEL), lambda i, j: (i, 0)),
                  pl.BlockSpec(g_post_ffn.shape, lambda i, j: (0, 0))],
        out_specs=pl.BlockSpec((TM_FFN, D_MODEL), lambda i, j: (i, 0)),
        out_shape=jax.ShapeDtypeStruct((M_ROWS, D_MODEL), F32),
        compiler_params=pltpu.CompilerParams(dimension_semantics=("arbitrary", "arbitrary"),
                                             vmem_limit_bytes=VMEM_LIMIT),
        name="ffn",
    )(h2, wg, wu, wd, x1, g_post_ffn)


NROW_DEC = DEC_SEQ * H_ATT
PAGE_ROWS = PAGE_SIZE * H_ATT
LF_ROWS = PAGE_ROWS // LANES
PAGES_PER_STEP = 16


def _dec_attn_kernel(pt_ref, q_ref, kn_ref, vn_ref, lfc_ref, lfr_ref, hmask_ref, *rest):
    del pt_ref
    n = PAGES_PER_STEP
    kp_refs, vp_refs, lfp_refs = rest[0:n], rest[n:2 * n], rest[2 * n:3 * n]
    o_ref, m_sc, l_sc, acc_sc, carry_sc, cn_sc = rest[3 * n:]
    j = pl.program_id(1)
    q = q_ref[0]

    def update(s, v_bf):
        m_prev = m_sc[...]
        m_new = jnp.maximum(m_prev, jnp.max(s, axis=1, keepdims=True))
        alpha = jnp.exp2(m_prev - m_new)
        p = jnp.exp2(s - m_new)
        l_sc[...] = alpha * l_sc[...] + jnp.sum(p, axis=1, keepdims=True)
        acc_sc[...] = alpha * acc_sc[...] + jnp.dot(p.astype(BF16), v_bf, preferred_element_type=F32)
        m_sc[...] = m_new

    @pl.when(j == 0)
    def _():
        r = lax.broadcasted_iota(jnp.int32, (NROW_DEC, NROW_DEC), 0)
        c = lax.broadcasted_iota(jnp.int32, (NROW_DEC, NROW_DEC), 1)
        same = (r & (H_ATT - 1)) == (c & (H_ATT - 1))
        keep = same & ((c >> 3) <= (r >> 3))
        cn_col = jnp.sum(jnp.where(keep, lfr_ref[0] * LOG2E, 0.0), axis=1, keepdims=True)
        cn_row = jnp.sum(jnp.where(same & ((r >> 3) <= (c >> 3)), lfc_ref[0] * LOG2E, 0.0),
                         axis=0, keepdims=True)
        m_sc[...] = jnp.full_like(m_sc, NEG_INF)
        l_sc[...] = jnp.zeros_like(l_sc)
        acc_sc[...] = jnp.zeros_like(acc_sc)
        carry_sc[...] = jnp.zeros_like(carry_sc)
        cn_sc[...] = cn_col
        s = _nt_dot(q, kn_ref[0].astype(BF16)) + cn_col - cn_row
        update(jnp.where(keep, s, NEG_INF), vn_ref[0].astype(BF16))

    lane = lax.broadcasted_iota(jnp.int32, (LF_ROWS, LANES), 1)
    rowi = lax.broadcasted_iota(jnp.int32, (LF_ROWS, LANES), 0)
    carry = carry_sc[...]
    base = cn_sc[...] + hmask_ref[...]
    scores = []
    for i in range(PAGES_PER_STEP):
        lf = lfp_refs[i][0] * LOG2E
        suf = lf
        tot = lf
        s_ = H_ATT
        while s_ < LANES:
            suf = suf + jnp.where(lane < LANES - s_, pltpu.roll(suf, LANES - s_, 1), 0.0)
            tot = tot + pltpu.roll(tot, s_, 1)
            s_ *= 2
        rsuf = tot
        s_ = 1
        while s_ < LF_ROWS:
            rsuf = rsuf + jnp.where(rowi < LF_ROWS - s_, pltpu.roll(rsuf, LF_ROWS - s_, 0), 0.0)
            s_ *= 2
        decay = carry + (suf - lf) + (rsuf - tot)
        carry = carry + rsuf[0:1, :]
        bias = jnp.concatenate([decay[r:r + 1, :] for r in range(LF_ROWS)], axis=1)
        k = kp_refs[i][0, 0].reshape(PAGE_ROWS, HD_ATT).astype(BF16)
        scores.append(_nt_dot(q, k) + (base + bias))
    carry_sc[...] = carry

    m_prev = m_sc[...]
    m_new = m_prev
    for s in scores:
        m_new = jnp.maximum(m_new, jnp.max(s, axis=1, keepdims=True))
    alpha = jnp.exp2(m_prev - m_new)
    l_new = alpha * l_sc[...]
    acc = alpha * acc_sc[...]
    for i, s in enumerate(scores):
        p = jnp.exp2(s - m_new)
        l_new = l_new + jnp.sum(p, axis=1, keepdims=True)
        v = vp_refs[i][0, 0].reshape(PAGE_ROWS, HD_ATT).astype(BF16)
        acc = acc + jnp.dot(p.astype(BF16), v, preferred_element_type=F32)
    l_sc[...] = l_new
    acc_sc[...] = acc
    m_sc[...] = m_new

    @pl.when(j == pl.num_programs(1) - 1)
    def _():
        o_ref[0] = acc_sc[...] * (1.0 / l_sc[...])


def _dec_attn(layer, page_flat, q_s, kn, vn, lf_col, lf_row, hmask, cache_k, cache_v, lfp):
    def page(i):
        return lambda b, j, pt: pt[b * N_PAGES + (N_PAGES - 1 - (j * PAGES_PER_STEP + i))]

    def kv_spec(i):
        pg = page(i)
        return pl.BlockSpec((1, 1, PAGE_SIZE, H_ATT, HD_ATT), lambda b, j, pt: (layer, pg(b, j, pt), 0, 0, 0))

    def lf_spec(i):
        pg = page(i)
        return pl.BlockSpec((1, LF_ROWS, LANES), lambda b, j, pt: (pg(b, j, pt), 0, 0))

    per_seq = lambda a: pl.BlockSpec((1,) + a.shape[1:], lambda b, j, pt: (b,) + (0,) * (a.ndim - 1))
    steps = range(PAGES_PER_STEP)
    grid_spec = pltpu.PrefetchScalarGridSpec(
        num_scalar_prefetch=1,
        grid=(DEC_BATCH, N_PAGES // PAGES_PER_STEP),
        in_specs=[per_seq(q_s), per_seq(kn), per_seq(vn), per_seq(lf_col), per_seq(lf_row),
                  pl.BlockSpec(hmask.shape, lambda b, j, pt: (0, 0))]
                 + [kv_spec(i) for i in steps] + [kv_spec(i) for i in steps] + [lf_spec(i) for i in steps],
        out_specs=pl.BlockSpec((1, NROW_DEC, HD_ATT), lambda b, j, pt: (b, 0, 0)),
        scratch_shapes=[pltpu.VMEM((NROW_DEC, 1), F32), pltpu.VMEM((NROW_DEC, 1), F32),
                        pltpu.VMEM((NROW_DEC, HD_ATT), F32), pltpu.VMEM((1, LANES), F32),
                        pltpu.VMEM((NROW_DEC, 1), F32)],
    )
    return pl.pallas_call(
        _dec_attn_kernel,
        grid_spec=grid_spec,
        out_shape=jax.ShapeDtypeStruct((DEC_BATCH, NROW_DEC, HD_ATT), F32),
        compiler_params=pltpu.CompilerParams(dimension_semantics=("arbitrary", "arbitrary"),
                                             vmem_limit_bytes=VMEM_LIMIT),
        name="dec_attn",
    )(page_flat, q_s, kn, vn, lf_col, lf_row, hmask, *([cache_k] * PAGES_PER_STEP),
      *([cache_v] * PAGES_PER_STEP), *([lfp] * PAGES_PER_STEP))


BC_W = 2 * G_SSM * N_STATE


def _dec_ssm_kernel(xsh_ref, bct_ref, cw_ref, cwt_ref, cb_ref, cbt_ref, dt_ref, alog_ref, z_ref,
                    dsk_ref, g_ref, st_ref, sto_ref, y_ref, y_sc):
    xc = cb_ref[...]
    bct = cbt_ref[...]
    for j in range(CONV_W):
        xc = xc + xsh_ref[0, j] * cw_ref[j:j + 1, :]
        bct = bct + bct_ref[0, j] * cwt_ref[:, j:j + 1]
    xs = _silu(xc)[:, 0:D_SSM]
    bct = _silu(bct)
    dtx = dt_ref[0]
    da = jnp.exp(dtx * (-jnp.exp(alog_ref[...])))
    xw = xs * dtx
    cols = {}
    for g in range(G_SSM):
        for t in range(DEC_SEQ):
            b0 = g * N_STATE
            c0 = (G_SSM + g) * N_STATE
            cols[g, t] = (jnp.broadcast_to(bct[b0:b0 + N_STATE, t:t + 1], (N_STATE, PAIR_W)),
                          jnp.broadcast_to(bct[c0:c0 + N_STATE, t:t + 1], (N_STATE, PAIR_W)))
    for pi in range(H_SSM // 2):
        g = (2 * pi) // R_SSM
        lanes = slice(pi * PAIR_W, (pi + 1) * PAIR_W)
        st = st_ref[0, 0, 2 * pi:2 * pi + 2].reshape(PAIR_W, N_STATE).T
        for t in range(DEC_SEQ):
            bb, cc = cols[g, t]
            st = st * da[t:t + 1, lanes] + bb * xw[t:t + 1, lanes]
            y_sc[t:t + 1, lanes] = jnp.sum(st * cc, axis=0, keepdims=True)
        sto_ref[0, 2 * pi:2 * pi + 2] = st.T.reshape(2, HD_SSM, N_STATE)
    y = (y_sc[...] + dsk_ref[...] * xs) * _silu(z_ref[0])
    half = D_SSM // G_SSM
    parts = []
    for g in range(G_SSM):
        yg = y[:, g * half:(g + 1) * half]
        parts.append(yg * lax.rsqrt(jnp.mean(yg * yg, axis=-1, keepdims=True) + EPS))
    y_ref[0] = jnp.concatenate(parts, axis=1) * g_ref[...]


def _dec_ssm(layer, xsh, bct, cw, cwt, cb, cbt, dt_x, alog_row, z_s, dsk_row, g_row, state_ssm):
    per_seq = lambda a: pl.BlockSpec((1,) + a.shape[1:], lambda b: (b,) + (0,) * (a.ndim - 1))
    full = lambda a: pl.BlockSpec(a.shape, lambda b: (0,) * a.ndim)
    return pl.pallas_call(
        _dec_ssm_kernel,
        grid=(DEC_BATCH,),
        in_specs=[per_seq(xsh), per_seq(bct), full(cw), full(cwt), full(cb), full(cbt), per_seq(dt_x),
                  full(alog_row), per_seq(z_s), full(dsk_row), full(g_row),
                  pl.BlockSpec((1, 1, H_SSM, HD_SSM, N_STATE), lambda b: (layer, b, 0, 0, 0))],
        out_specs=(pl.BlockSpec((1, H_SSM, HD_SSM, N_STATE), lambda b: (b, 0, 0, 0)),
                   pl.BlockSpec((1, DEC_SEQ, D_SSM), lambda b: (b, 0, 0))),
        out_shape=(jax.ShapeDtypeStruct((DEC_BATCH, H_SSM, HD_SSM, N_STATE), F32),
                   jax.ShapeDtypeStruct((DEC_BATCH, DEC_SEQ, D_SSM), F32)),
        scratch_shapes=[pltpu.VMEM((DEC_SEQ, D_SSM), F32)],
        compiler_params=pltpu.CompilerParams(dimension_semantics=("arbitrary",),
                                             vmem_limit_bytes=VMEM_LIMIT),
        name="dec_ssm",
    )(xsh, bct, cw, cwt, cb, cbt, dt_x, alog_row, z_s, dsk_row, g_row, state_ssm)


def _small_row(*pieces):
    v = jnp.concatenate([p.astype(F32) for p in pieces])
    return jnp.pad(v, (0, SMALL_W - v.shape[0]))[None, :]


def kernel(x_prompt, x_sample, cache_k, cache_v, cache_logf, state_ssm, state_conv, page_table, meta,
           w_in, b_f, dt_bias, a_log, d_skip, conv_w, conv_b, g_pre_mix, g_post_mix, g_att_out, g_ssm_out,
           w_out, g_pre_ffn, g_post_ffn, w_gate, w_up, w_down):
    assert x_prompt.shape == (BATCH, SEQ, D_MODEL) and x_sample.shape == (DEC_BATCH, DEC_SEQ, D_MODEL)
    assert page_table.shape == (DEC_BATCH, N_PAGES)
    n_pool = cache_k.shape[1]

    xp = jnp.concatenate([jnp.broadcast_to(meta.astype(F32)[None], (BATCH, N_META, D_MODEL)), x_prompt,
                          jnp.zeros((BATCH, LP - L_VALID, D_MODEL), F32)], axis=1)
    x = xp.reshape(M_ROWS, D_MODEL)
    x = lax.dynamic_update_slice(x, x_sample.reshape(N_SAMPLE, D_MODEL), (SAMPLE_ROW0, 0))

    page_flat = page_table.reshape(-1).astype(jnp.int32)
    lfp = cache_logf.reshape(DEPTH, n_pool, LF_ROWS, LANES)
    hmask = jnp.where((jnp.arange(NROW_DEC)[:, None] % H_ATT) == (jnp.arange(PAGE_ROWS)[None, :] % H_ATT),
                      0.0, NEG_INF).astype(F32)

    srows = slice(SAMPLE_ROW0, SAMPLE_ROW0 + N_SAMPLE)

    assert w_in.shape == (DEPTH, D_MODEL, D_IN)
    w_in_bf, w_out_bf, w_gate_bf, w_up_bf, w_down_bf = (w.astype(BF16) for w in (w_in, w_out, w_gate, w_up, w_down))

    e_head, e_chunk = _head_expansion(HD_SSM), _head_expansion(Q_SSD)

    outs = {k: [] for k in ("k_p", "v_p", "lf_p", "k_s", "v_s", "lf_s", "ssm_p", "ssm_s", "conv_p", "conv_s")}
    for i in range(DEPTH):
        bias = _small_row(b_f[i], dt_bias[i])
        q_bf, k_bf, vt_bf, k_p, v_p, k_s, v_s, z, xbc, small, fcs = _inproj(
            i, x, g_pre_mix[i][None, :], w_in_bf, bias)

        att = _attn_prompt(q_bf, k_bf, vt_bf, fcs)
        alog_row = jnp.pad(a_log[i].astype(F32), (DT_LANE0, SMALL_W - DT_LANE0 - H_SSM))[None, :]
        dsk_row = jnp.repeat(d_skip[i].astype(F32), HD_SSM)[None, :]
        g_ssm_row = g_ssm_out[i].astype(F32)[None, :]
        y_ssm, st_p = _ssd_prompt(xbc, small, z, conv_w[i], conv_b[i][None, :], alog_row, dsk_row, g_ssm_row,
                                  e_head, e_chunk)

        q_s = q_bf[srows].reshape(DEC_BATCH, NROW_DEC, HD_ATT)
        kn = k_s.reshape(DEC_BATCH, NROW_DEC, HD_ATT)
        vn = v_s.reshape(DEC_BATCH, NROW_DEC, HD_ATT)
        lf_s = small[srows, :H_ATT].reshape(DEC_BATCH, DEC_SEQ, H_ATT)
        att_s = _dec_attn(i, page_flat, q_s, kn, vn, lf_s.reshape(DEC_BATCH, NROW_DEC, 1),
                          lf_s.reshape(DEC_BATCH, 1, NROW_DEC), hmask, cache_k, cache_v, lfp[i])
        xbc_s = xbc[srows].reshape(DEC_BATCH, DEC_SEQ, C_CONV)
        xpad = jnp.concatenate([state_conv[i].astype(F32), xbc_s], axis=1)
        xsh = jnp.stack([xpad[:, j:j + DEC_SEQ] for j in range(CONV_W)], axis=1)
        bct = xsh[..., D_SSM:].transpose(0, 1, 3, 2)
        dt_s = small[srows, DT_LANE0:DT_LANE0 + H_SSM].reshape(DEC_BATCH, DEC_SEQ, H_SSM)
        dt_x = jnp.repeat(dt_s, HD_SSM, axis=2)
        alog_x = jnp.repeat(a_log[i].astype(F32), HD_SSM)[None, :]
        z_s = z[srows].reshape(DEC_BATCH, DEC_SEQ, D_SSM)
        st_s, y_s = _dec_ssm(i, xsh, bct, conv_w[i], conv_w[i][:, D_SSM:].T, conv_b[i][None, :],
                             conv_b[i][D_SSM:, None], dt_x, alog_x, z_s, dsk_row, g_ssm_row, state_ssm)

        x1, h2 = _outproj(i, att, y_ssm, att_s.reshape(N_SAMPLE, D_ATT), y_s.reshape(N_SAMPLE, D_SSM), x,
                          w_out_bf, g_att_out[i][None, :], g_post_mix[i][None, :], g_pre_ffn[i][None, :])
        x = _ffn(i, h2, w_gate_bf, w_up_bf, w_down_bf, x1, g_post_ffn[i][None, :])

        outs["k_p"].append(k_p.reshape(BATCH, L_VALID, H_ATT, HD_ATT))
        outs["v_p"].append(v_p.reshape(BATCH, L_VALID, H_ATT, HD_ATT))
        outs["lf_p"].append(small[:, :H_ATT].reshape(BATCH, LP, H_ATT)[:, :L_VALID])
        outs["k_s"].append(kn.reshape(DEC_BATCH, DEC_SEQ, H_ATT, HD_ATT))
        outs["v_s"].append(vn.reshape(DEC_BATCH, DEC_SEQ, H_ATT, HD_ATT))
        outs["lf_s"].append(lf_s)
        outs["ssm_p"].append(st_p)
        outs["ssm_s"].append(st_s)
        outs["conv_p"].append(xbc.reshape(BATCH, LP, C_CONV)[:, L_VALID - (CONV_W - 1):L_VALID])
        outs["conv_s"].append(xpad[:, DEC_SEQ:])

    xf = x.reshape(BATCH, LP, D_MODEL)
    y_prompt = xf[:, N_META:L_VALID]
    y_sample = x[srows].reshape(DEC_BATCH, DEC_SEQ, D_MODEL)
    st = lambda k: jnp.stack(outs[k])
    return (y_prompt, y_sample, st("k_p"), st("v_p"), st("lf_p"), st("k_s"), st("v_s"), st("lf_s"),
            st("ssm_p"), st("ssm_s"), st("conv_p"), st("conv_s"))
```
